```python
import jax, jax.numpy as jnp
from jax import lax
import numpy as np

D_MODEL = 2048
BATCH = 8
SEQ = 2048
DEPTH = 2
DEC_BATCH = 128
DEC_SEQ = 4
PAST_LEN = 16384
PAGE_SIZE = 128

MIX_WIDTH = D_MODEL
MLA_HEADS = MIX_WIDTH // 256
MLA_NOPE = 128
MLA_ROPE = 64
MLA_V = 128
MLA_Q_LORA = 3 * D_MODEL // 16
MLA_KV_LORA = D_MODEL // 8
MLA_SCALE = (MLA_NOPE + MLA_ROPE) ** -0.5
ROPE_THETA = 10000.0
DN_HEADS = MIX_WIDTH // 512
DN_DK = 128
DN_DV = 128
DN_CONV = 4
DN_CHUNK = 64
DN_QKV = 2 * DN_HEADS * DN_DK + DN_HEADS * DN_DV
SB_HEADS = MIX_WIDTH // 512
SB_D = 128
SB_SCALE = SB_D ** -0.5
Q_BLOCK = 128
D_FF = 4 * D_MODEL
NORM_EPS = 1e-6
L2_EPS = 1e-6
NEG_BIG = -1e30
IN_WIDTH = (MLA_Q_LORA + MLA_KV_LORA + MLA_ROPE + DN_QKV + DN_HEADS * DN_DV + 2 * DN_HEADS
            + SB_HEADS * SB_D + 2 * SB_D)

kernel_name = 'hybrid_mla_gdn_stickbreak_decoder_step'


def rmsnorm(x, g):
    xf = x.astype(jnp.float32)
    y = xf * lax.rsqrt(jnp.mean(xf * xf, axis=-1, keepdims=True) + NORM_EPS)
    return (y * g.astype(jnp.float32)).astype(x.dtype)


def l2norm(x):
    return x * lax.rsqrt(jnp.sum(x * x, axis=-1, keepdims=True) + L2_EPS)


def rope(x, pos):
    half = x.shape[-1] // 2
    inv_freq = ROPE_THETA ** (-jnp.arange(half, dtype=jnp.float32) / half)
    ang = pos.astype(jnp.float32)[:, None] * inv_freq[None, :]
    shape = (1, ang.shape[0]) + (1,) * (x.ndim - 3) + (half,)
    cos = jnp.cos(ang).reshape(shape)
    sin = jnp.sin(ang).reshape(shape)
    xf = x.astype(jnp.float32)
    x1, x2 = xf[..., :half], xf[..., half:]
    return jnp.concatenate([x1 * cos - x2 * sin, x2 * cos + x1 * sin], axis=-1).astype(x.dtype)


def split_columns(proj):
    sizes = (MLA_Q_LORA, MLA_KV_LORA, MLA_ROPE, DN_QKV, DN_HEADS * DN_DV, DN_HEADS, DN_HEADS,
             SB_HEADS * SB_D, SB_D, SB_D)
    parts, off = [], 0
    for n in sizes:
        parts.append(proj[..., off:off + n])
        off += n
    return parts


def blocked_causal(attend, qs, past_segs, new_kv, n_past):
    T = qs[0].shape[1]
    outs = []
    for start in range(0, T, Q_BLOCK):
        end = min(start + Q_BLOCK, T)
        q_pos = n_past + jnp.arange(start, end)
        new_seg = tuple(a[:, :end] for a in new_kv) + (n_past + jnp.arange(end),)
        outs.append(attend(tuple(a[:, start:end] for a in qs), past_segs + [new_seg], q_pos))
    return jnp.concatenate(outs, axis=1)


def mla_attend(qs, segs, q_pos):
    q_lat, q_rope = qs
    s = jnp.concatenate([jnp.einsum('bqhr,bsr->bhqs', q_lat, c) + jnp.einsum('bqhd,bsd->bhqs', q_rope, r)
                         for c, r, _ in segs], axis=-1).astype(jnp.float32) * MLA_SCALE
    k_pos = jnp.concatenate([p for _, _, p in segs])
    s = jnp.where(k_pos[None, :] <= q_pos[:, None], s, NEG_BIG)
    p = jax.nn.softmax(s, axis=-1)
    out, off = None, 0
    for c, _, _ in segs:
        n = c.shape[1]
        o = jnp.einsum('bhqs,bsr->bqhr', p[..., off:off + n].astype(c.dtype), c)
        out = o if out is None else out + o
        off += n
    return out


def sb_attend(qs, segs, q_pos):
    (q,) = qs
    z = jnp.concatenate([jnp.einsum('bqhd,bsd->bhqs', q, k) for k, _, _ in segs],
                        axis=-1).astype(jnp.float32) * SB_SCALE
    k_pos = jnp.concatenate([p for _, _, p in segs])
    mask = k_pos[None, :] < q_pos[:, None]
    log_keep = jnp.where(mask, jax.nn.log_sigmoid(-z), 0.0)
    log_after = lax.cumsum(log_keep, axis=z.ndim - 1, reverse=True) - log_keep
    A = jnp.where(mask, jnp.exp(jax.nn.log_sigmoid(z) + log_after), 0.0)
    out, off = None, 0
    for k, v, _ in segs:
        n = k.shape[1]
        o = jnp.einsum('bhqs,bsd->bqhd', A[..., off:off + n].astype(v.dtype), v)
        out = o if out is None else out + o
        off += n
    return out


def causal_depthwise_conv(x_ext, w):
    C = x_ext.shape[-1]
    return lax.conv_general_dilated(x_ext, w.astype(x_ext.dtype).reshape(DN_CONV, 1, C),
                                    window_strides=(1,), padding='VALID',
                                    dimension_numbers=('NWC', 'WIO', 'NWC'), feature_group_count=C)


def gated_delta_chunked(q, k, v, g, beta, S0):
    B, T, H, _ = q.shape
    DV = v.shape[-1]
    C = min(DN_CHUNK, T)
    pad = (-T) % C
    if pad:
        padf = lambda t: jnp.pad(t, [(0, 0), (0, pad)] + [(0, 0)] * (t.ndim - 2))
        q, k, v, g, beta = padf(q), padf(k), padf(v), padf(g), padf(beta)
    N = (T + pad) // C

    def chunked(t):
        return jnp.moveaxis(t.reshape((B, N, C, H) + t.shape[3:]), (1, 3), (0, 2))

    qc, kc, vc, gc, bc = chunked(q), chunked(k), chunked(v), chunked(g), chunked(beta)
    G = jnp.cumsum(gc, axis=-1)
    idx = jnp.arange(C)
    incl = idx[:, None] >= idx[None, :]
    strict = idx[:, None] > idx[None, :]
    D = jnp.exp(jnp.where(incl, G[..., :, None] - G[..., None, :], -jnp.inf))
    A = jnp.where(strict, bc[..., :, None] * jnp.einsum('...id,...jd->...ij', kc, kc) * D, 0.0)
    Tm = jnp.eye(C, dtype=A.dtype) + A
    U = lax.linalg.triangular_solve(Tm, bc[..., None] * vc, left_side=True, lower=True, unit_diagonal=True)
    W = lax.linalg.triangular_solve(Tm, (bc * jnp.exp(G))[..., None] * kc, left_side=True, lower=True,
                                    unit_diagonal=True)
    QK = jnp.einsum('...id,...jd->...ij', qc, kc) * D
    q_dec = qc * jnp.exp(G)[..., None]
    k_dec = kc * jnp.exp(G[..., -1:] - G)[..., None]
    g_last = jnp.exp(G[..., -1])

    def step(S, xs):
        U_, W_, QK_, qd, kd, gl = xs
        delta = U_ - jnp.einsum('bhck,bhkv->bhcv', W_, S)
        o = jnp.einsum('bhck,bhkv->bhcv', qd, S) + jnp.einsum('bhij,bhjv->bhiv', QK_, delta)
        S = gl[..., None, None] * S + jnp.einsum('bhck,bhcv->bhkv', kd, delta)
        return S, o

    S, o = lax.scan(step, S0, (U, W, QK, q_dec, k_dec, g_last))
    o = jnp.moveaxis(o, (0, 2), (1, 3)).reshape(B, N * C, H, DV)[:, :T]
    return o, S


def trunk_layer(x, n_past, past, w):
    B, T, _ = x.shape
    f32 = jnp.float32
    pos = n_past + jnp.arange(T)
    h = rmsnorm(x, w['pre_mix'])
    cq, ckv, kr, qkv, z, a, b, sq, sk, sv = split_columns(h @ w['in'])
    if past is None:
        past_mla, past_sb = [], []
        S0 = jnp.zeros((B, DN_HEADS, DN_DK, DN_DV), f32)
        conv_buf = jnp.zeros((B, DN_CONV - 1, DN_QKV), qkv.dtype)
    else:
        lat_p, rope_p, k_p, v_p, S0, conv_buf = past
        past_pos = jnp.arange(lat_p.shape[1])
        past_mla = [(lat_p, rope_p, past_pos)]
        past_sb = [(k_p, v_p, past_pos)]

    cq = rmsnorm(cq, w['mla_q_norm'])
    q = jnp.einsum('btr,rhe->bthe', cq, w['mla_uq'])
    q_rope = rope(q[..., MLA_NOPE:], pos)
    q_lat = jnp.einsum('bthn,rhn->bthr', q[..., :MLA_NOPE], w['mla_uk'])
    ckv = rmsnorm(ckv, w['mla_kv_norm'])
    kr = rope(kr, pos)
    o_lat = blocked_causal(mla_attend, (q_lat, q_rope), past_mla, (ckv, kr), n_past)
    o_mla = rmsnorm(jnp.einsum('bthr,rhv->bthv', o_lat, w['mla_uv']), w['mla_out_norm'])

    qkv_ext = jnp.concatenate([conv_buf.astype(qkv.dtype), qkv], axis=1)
    new_conv = qkv_ext[:, qkv_ext.shape[1] - (DN_CONV - 1):]
    qkv_c = jax.nn.silu(causal_depthwise_conv(qkv_ext, w['dn_conv']).astype(f32))
    hk = DN_HEADS * DN_DK
    dq = l2norm(qkv_c[..., :hk].reshape(B, T, DN_HEADS, DN_DK)) * DN_DK ** -0.5
    dk = l2norm(qkv_c[..., hk:2 * hk].reshape(B, T, DN_HEADS, DN_DK))
    dv = qkv_c[..., 2 * hk:].reshape(B, T, DN_HEADS, DN_DV)
    beta = jax.nn.sigmoid(b.astype(f32))
    g = -jnp.exp(w['dn_A_log'].astype(f32)) * jax.nn.softplus(a.astype(f32) + w['dn_dt_bias'].astype(f32))
    o_dn, S_new = gated_delta_chunked(dq, dk, dv, g, beta, S0.astype(f32))
    gate = jax.nn.silu(z.astype(f32).reshape(B, T, DN_HEADS, DN_DV))
    o_dn = (rmsnorm(o_dn, w['dn_out_norm']) * gate).astype(x.dtype)

    sq = sq.reshape(B, T, SB_HEADS, SB_D)
    o_sb = rmsnorm(blocked_causal(sb_attend, (sq,), past_sb, (sk, sv), n_past), w['sb_out_norm'])

    mixed = jnp.concatenate([o_mla.reshape(B, T, -1).astype(x.dtype), o_dn.reshape(B, T, -1),
                             o_sb.reshape(B, T, -1).astype(x.dtype)], axis=-1) @ w['out']
    x = x + rmsnorm(mixed, w['post_mix'])
    hm = rmsnorm(x, w['pre_mlp'])
    ff = jnp.square(jax.nn.relu(hm @ w['up'])) @ w['down']
    x = x + rmsnorm(ff, w['post_mlp'])
    return x, (ckv, kr, sk, sv, S_new.astype(x.dtype), new_conv)


def gather_pages(cache_l, page_table):
    rows = cache_l[page_table]
    return rows.reshape(page_table.shape[0], -1, rows.shape[-1])


def setup_inputs(seed: int = 0) -> dict:
    key = jax.random.key(seed)
    ks = jax.random.split(key, 48)
    keys = [ks[i] for i in range(48)]

    def nrm(shape, scale):
        return jax.random.normal(keys.pop(), shape, jnp.float32) * scale

    def gain(shape):
        return 1.0 + 0.05 * jax.random.normal(keys.pop(), shape, jnp.float32)

    n_pages = PAST_LEN // PAGE_SIZE
    n_used = DEC_BATCH * n_pages
    n_pool = n_used + max(1, n_used // 4)
    page_table = jax.random.permutation(keys.pop(), n_pool)[:n_used].reshape(DEC_BATCH, n_pages).astype(jnp.int32)
    A_log = jnp.log(jax.random.uniform(keys.pop(), (DEPTH, DN_HEADS), jnp.float32, 1.0, 16.0))
    dt = jnp.exp(jax.random.uniform(keys.pop(), (DEPTH, DN_HEADS), jnp.float32,
                                    float(np.log(1e-3)), float(np.log(1e-1))))
    dt_bias = jnp.log(jnp.expm1(dt))
    return {
        'x_prompt': nrm((BATCH, SEQ, D_MODEL), 1.0),
        'x_sample': nrm((DEC_BATCH, DEC_SEQ, D_MODEL), 1.0),
        'cache_mla_latent': nrm((DEPTH, n_pool, PAGE_SIZE, MLA_KV_LORA), 1.0),
        'cache_mla_rope': nrm((DEPTH, n_pool, PAGE_SIZE, MLA_ROPE), 1.0),
        'cache_sb_k': nrm((DEPTH, n_pool, PAGE_SIZE, SB_D), 1.0),
        'cache_sb_v': nrm((DEPTH, n_pool, PAGE_SIZE, SB_D), 1.0),
        'state_dn_S': nrm((DEPTH, DEC_BATCH, DN_HEADS, DN_DK, DN_DV), DN_DK ** -0.5),
        'state_dn_conv': nrm((DEPTH, DEC_BATCH, DN_CONV - 1, DN_QKV), 1.0),
        'page_table': page_table,
        'w_in': nrm((DEPTH, D_MODEL, IN_WIDTH), D_MODEL ** -0.5),
        'g_pre_mix': gain((DEPTH, D_MODEL)),
        'g_mla_q': gain((DEPTH, MLA_Q_LORA)),
        'w_mla_uq': nrm((DEPTH, MLA_Q_LORA, MLA_HEADS, MLA_NOPE + MLA_ROPE), MLA_Q_LORA ** -0.5),
        'g_mla_kv': gain((DEPTH, MLA_KV_LORA)),
        'w_mla_uk': nrm((DEPTH, MLA_KV_LORA, MLA_HEADS, MLA_NOPE), MLA_KV_LORA ** -0.5),
        'w_mla_uv': nrm((DEPTH, MLA_KV_LORA, MLA_HEADS, MLA_V), MLA_KV_LORA ** -0.5),
        'g_mla_out': gain((DEPTH, MLA_HEADS, MLA_V)),
        'w_dn_conv': nrm((DEPTH, DN_CONV, DN_QKV), DN_CONV ** -0.5),
        'dn_A_log': A_log,
        'dn_dt_bias': dt_bias,
        'g_dn_out': gain((DEPTH, DN_DV)),
        'g_sb_out': gain((DEPTH, SB_HEADS, SB_D)),
        'w_out': nrm((DEPTH, MIX_WIDTH, D_MODEL), MIX_WIDTH ** -0.5),
        'g_post_mix': gain((DEPTH, D_MODEL)),
        'g_pre_mlp': gain((DEPTH, D_MODEL)),
        'w_up': nrm((DEPTH, D_MODEL, D_FF), D_MODEL ** -0.5),
        'w_down': nrm((DEPTH, D_FF, D_MODEL), D_FF ** -0.5),
        'g_post_mlp': gain((DEPTH, D_MODEL)),
    }


def reference(x_prompt, x_sample, cache_mla_latent, cache_mla_rope, cache_sb_k, cache_sb_v,
              state_dn_S, state_dn_conv, page_table, w_in, g_pre_mix, g_mla_q, w_mla_uq, g_mla_kv,
              w_mla_uk, w_mla_uv, g_mla_out, w_dn_conv, dn_A_log, dn_dt_bias, g_dn_out, g_sb_out,
              w_out, g_post_mix, g_pre_mlp, w_up, w_down, g_post_mlp):
    y_p, y_s = x_prompt, x_sample
    outs_p, outs_s = [], []
    for l in range(DEPTH):
        w = {'in': w_in[l], 'pre_mix': g_pre_mix[l], 'mla_q_norm': g_mla_q[l], 'mla_uq': w_mla_uq[l],
             'mla_kv_norm': g_mla_kv[l], 'mla_uk': w_mla_uk[l], 'mla_uv': w_mla_uv[l],
             'mla_out_norm': g_mla_out[l], 'dn_conv': w_dn_conv[l], 'dn_A_log': dn_A_log[l],
             'dn_dt_bias': dn_dt_bias[l], 'dn_out_norm': g_dn_out[l], 'sb_out_norm': g_sb_out[l],
             'out': w_out[l], 'post_mix': g_post_mix[l], 'pre_mlp': g_pre_mlp[l], 'up': w_up[l],
             'down': w_down[l], 'post_mlp': g_post_mlp[l]}
        y_p, st_p = trunk_layer(y_p, 0, None, w)
        past = (gather_pages(cache_mla_latent[l], page_table), gather_pages(cache_mla_rope[l], page_table),
                gather_pages(cache_sb_k[l], page_table), gather_pages(cache_sb_v[l], page_table),
                state_dn_S[l], state_dn_conv[l])
        y_s, st_s = trunk_layer(y_s, past[0].shape[1], past, w)
        outs_p.append(st_p)
        outs_s.append(st_s)
    p_lat, p_rope, p_sbk, p_sbv, p_S, p_conv = (jnp.stack(t) for t in zip(*outs_p))
    s_lat, s_rope, s_sbk, s_sbv, s_S, s_conv = (jnp.stack(t) for t in zip(*outs_s))
    return (y_p, y_s, p_lat, p_rope, p_sbk, p_sbv, p_S, p_conv, s_lat, s_rope, s_sbk, s_sbv, s_S, s_conv)
```

```python
import functools
import math

import jax
import jax.numpy as jnp
from jax import lax
from jax.experimental import pallas as pl
from jax.experimental.pallas import tpu as pltpu

F32 = jnp.float32
BF16 = jnp.bfloat16

MLA_NOPE = 128
MLA_ROPE = 64
MLA_V = 128
DN_DK = 128
DN_DV = 128
DN_CONV = 4
DN_CHUNK = 64
SB_D = 128
NORM_EPS = 1e-6
L2_EPS = 1e-6
NEG_BIG = -1e30
ROPE_THETA = 10000.0

LANES = 128
SUBLANES = 8
VMEM_LIMIT = 48 * 1024 * 1024

OFF_QKV, W_QKV = 0, 1536
OFF_CQ, W_CQ = 1536, 384
OFF_SK = 1920
OFF_Z, W_Z = 2048, 512
OFF_SQ, W_SQ = 2560, 512
OFF_CKV, W_CKV = 3072, 256
OFF_SV = 3328
OFF_KR = 3456
OFF_KRR = 3584
OFF_AB = 3712
NP_IN = 3840
QK_W = 384

HIGHEST = lax.Precision.HIGHEST
NT_DIMS = (((1,), (1,)), ((), ()))
TN_DIMS = (((0,), (0,)), ((), ()))


def _cparams(sem):
    return pltpu.CompilerParams(dimension_semantics=sem, vmem_limit_bytes=VMEM_LIMIT)


def _rms(x, g):
    return x * lax.rsqrt(jnp.mean(x * x, axis=-1, keepdims=True) + NORM_EPS) * g


def _dot(a, b):
    return jnp.dot(a, b, preferred_element_type=F32)


def _dot_nt(a, b):
    return lax.dot_general(a, b, NT_DIMS, preferred_element_type=F32)


def _dotx(a, b):
    return jnp.dot(a, b, preferred_element_type=F32, precision=HIGHEST)


def _dotx_nt(a, b):
    return lax.dot_general(a, b, NT_DIMS, preferred_element_type=F32, precision=HIGHEST)


def _dotx_tn(a, b):
    return lax.dot_general(a, b, TN_DIMS, preferred_element_type=F32, precision=HIGHEST)


def _log_sigmoid(z):
    return jnp.minimum(z, 0.0) - jnp.log1p(jnp.exp(-jnp.abs(z)))


def _softplus(x):
    return jnp.maximum(x, 0.0) + jnp.log1p(jnp.exp(-jnp.abs(x)))


def _pick(total, want):
    t = min(total, want)
    assert total % t == 0, (total, want)
    return t


def _norm_cast_kernel(x_ref, g_ref, o_ref):
    o_ref[...] = _rms(x_ref[...], g_ref[...]).astype(o_ref.dtype)


def norm_cast(x, g):
    M, D = x.shape
    tm = _pick(M, 512)
    return pl.pallas_call(
        _norm_cast_kernel, name="norm_cast",
        grid=(M // tm,),
        in_specs=[pl.BlockSpec((tm, D), lambda i: (i, 0)), pl.BlockSpec((1, D), lambda i: (0, 0))],
        out_specs=pl.BlockSpec((tm, D), lambda i: (i, 0)),
        out_shape=jax.ShapeDtypeStruct((M, D), BF16),
        compiler_params=_cparams(("parallel",)),
    )(x, g.reshape(1, D))


def _mm_kernel(a_ref, w_ref, o_ref, *, relu2):
    acc = _dot(a_ref[...], w_ref[...])
    if relu2:
        acc = jnp.square(jnp.maximum(acc, 0.0))
    o_ref[...] = acc.astype(o_ref.dtype)


def matmul(a, w, out_dtype, *, relu2=False, tm=1024, tn=1024):
    M, K = a.shape
    N = w.shape[1]
    tm, tn = _pick(M, tm), _pick(N, tn)
    return pl.pallas_call(
        functools.partial(_mm_kernel, relu2=relu2), name="matmul_relu2" if relu2 else "matmul",
        grid=(M // tm, N // tn),
        in_specs=[pl.BlockSpec((tm, K), lambda i, j: (i, 0)), pl.BlockSpec((K, tn), lambda i, j: (0, j))],
        out_specs=pl.BlockSpec((tm, tn), lambda i, j: (i, j)),
        out_shape=jax.ShapeDtypeStruct((M, N), out_dtype),
        compiler_params=_cparams(("parallel", "arbitrary")),
    )(a, w)


def _mm_res_norm_kernel(a_ref, w_ref, x_ref, gp_ref, gn_ref, xo_ref, ho_ref, acc_ref):
    k = pl.program_id(1)

    @pl.when(k == 0)
    def _():
        acc_ref[...] = jnp.zeros_like(acc_ref)

    acc_ref[...] += _dot(a_ref[...], w_ref[...])

    @pl.when(k == pl.num_programs(1) - 1)
    def _():
        xn = x_ref[...] + _rms(acc_ref[...], gp_ref[...])
        xo_ref[...] = xn
        ho_ref[...] = _rms(xn, gn_ref[...]).astype(ho_ref.dtype)


def matmul_res_norm(a, w, x, g_post, g_next, *, tm=512, tk=1024):
    M, K = a.shape
    N = w.shape[1]
    tm, tk = _pick(M, tm), _pick(K, tk)
    return pl.pallas_call(
        _mm_res_norm_kernel, name="matmul_res_norm",
        grid=(M // tm, K // tk),
        in_specs=[
            pl.BlockSpec((tm, tk), lambda i, k: (i, k)),
            pl.BlockSpec((tk, N), lambda i, k: (k, 0)),
            pl.BlockSpec((tm, N), lambda i, k: (i, 0)),
            pl.BlockSpec((1, N), lambda i, k: (0, 0)),
            pl.BlockSpec((1, N), lambda i, k: (0, 0)),
        ],
        out_specs=[pl.BlockSpec((tm, N), lambda i, k: (i, 0)), pl.BlockSpec((tm, N), lambda i, k: (i, 0))],
        out_shape=[jax.ShapeDtypeStruct((M, N), F32), jax.ShapeDtypeStruct((M, N), BF16)],
        scratch_shapes=[pltpu.VMEM((tm, N), F32)],
        compiler_params=_cparams(("parallel", "arbitrary")),
    )(a, w, x, g_post.reshape(1, N), g_next.reshape(1, N))


def _mla_prep_kernel(cq_ref, ckv_ref, kr_ref, krr_ref, cos_ref, sin_ref, gq_ref, gkv_ref, uq_ref, ukt_ref,
                     q_ref, kc_ref, ckvo_ref, kro_ref, *, heads, scale):
    cqn = _rms(cq_ref[...], gq_ref[...]).astype(BF16)
    q = _dot(cqn, uq_ref[...])
    cos, sin = cos_ref[...], sin_ref[...]
    hw = heads * LANES
    for h in range(heads):
        lo, hi = h * LANES, (h + 1) * LANES
        q_lat = _dot(q[:, lo:hi].astype(BF16), ukt_ref[h])
        q_rope = q[:, hw + lo:hw + hi] * cos + q[:, 2 * hw + lo:2 * hw + hi] * sin
        q_ref[h, :, 0:W_CKV] = (q_lat * scale).astype(BF16)
        q_ref[h, :, W_CKV:QK_W] = (q_rope * scale).astype(BF16)
    c = _rms(ckv_ref[...], gkv_ref[...])
    kr = kr_ref[...] * cos + krr_ref[...] * sin
    ckvo_ref[...] = c
    kro_ref[...] = kr[:, :MLA_ROPE]
    kc_ref[:, 0:W_CKV] = c.astype(BF16)
    kc_ref[:, W_CKV:QK_W] = kr.astype(BF16)


def mla_prep(proj, cos, sin, g_q, g_kv, uq_all, ukt, *, tm=256):
    M = proj.shape[0]
    heads = ukt.shape[0]
    tm = _pick(M, tm)
    scale = float((MLA_NOPE + MLA_ROPE) ** -0.5)
    col = lambda off, w: (lambda i: (i, off // w))
    return pl.pallas_call(
        functools.partial(_mla_prep_kernel, heads=heads, scale=scale), name="mla_prep",
        grid=(M // tm,),
        in_specs=[
            pl.BlockSpec((tm, W_CQ), col(OFF_CQ, W_CQ)),
            pl.BlockSpec((tm, W_CKV), col(OFF_CKV, W_CKV)),
            pl.BlockSpec((tm, LANES), col(OFF_KR, LANES)),
            pl.BlockSpec((tm, LANES), col(OFF_KRR, LANES)),
            pl.BlockSpec((tm, LANES), lambda i: (i, 0)),
            pl.BlockSpec((tm, LANES), lambda i: (i, 0)),
            pl.BlockSpec((1, W_CQ), lambda i: (0, 0)),
            pl.BlockSpec((1, W_CKV), lambda i: (0, 0)),
            pl.BlockSpec(uq_all.shape, lambda i: (0, 0)),
            pl.BlockSpec(ukt.shape, lambda i: (0, 0, 0)),
        ],
        out_specs=[
            pl.BlockSpec((heads, tm, QK_W), lambda i: (0, i, 0)),
            pl.BlockSpec((tm, QK_W), lambda i: (i, 0)),
            pl.BlockSpec((tm, W_CKV), lambda i: (i, 0)),
            pl.BlockSpec((tm, MLA_ROPE), lambda i: (i, 0)),
        ],
        out_shape=[
            jax.ShapeDtypeStruct((heads, M, QK_W), BF16),
            jax.ShapeDtypeStruct((M, QK_W), BF16),
            jax.ShapeDtypeStruct((M, W_CKV), F32),
            jax.ShapeDtypeStruct((M, MLA_ROPE), F32),
        ],
        compiler_params=_cparams(("parallel",)),
    )(proj, proj, proj, proj, cos, sin, g_q.reshape(1, -1), g_kv.reshape(1, -1), uq_all, ukt)


def _softmax_step(s, v, m_prev, l_prev, acc_prev):
    m_new = jnp.maximum(m_prev, jnp.max(s, axis=-1, keepdims=True))
    alpha = jnp.exp(m_prev - m_new)
    p = jnp.exp(s - m_new)
    l_new = alpha * l_prev + jnp.sum(p, axis=-1, keepdims=True)
    acc_new = alpha * acc_prev + _dot(p.astype(BF16), v)
    return m_new, l_new, acc_new


def _mla_prompt_kernel(q_ref, k_ref, o_ref, m_sc, l_sc, acc_sc, *, heads, tq, tk):
    qi, ki = pl.program_id(1), pl.program_id(2)
    rows = heads * tq

    @pl.when(ki == 0)
    def _():
        m_sc[...] = jnp.full_like(m_sc, NEG_BIG)
        l_sc[...] = jnp.zeros_like(l_sc)
        acc_sc[...] = jnp.zeros_like(acc_sc)

    @pl.when(ki * tk < (qi + 1) * tq)
    def _():
        q = q_ref[...].reshape(rows, QK_W)
        k = k_ref[...]
        s = _dot_nt(q, k).reshape(heads, tq, tk)
        q_pos = qi * tq + lax.broadcasted_iota(jnp.int32, (tq, tk), 0)
        k_pos = ki * tk + lax.broadcasted_iota(jnp.int32, (tq, tk), 1)
        s = jnp.where((k_pos <= q_pos)[None], s, NEG_BIG).reshape(rows, tk)
        m, l, acc = _softmax_step(s, k[:, 0:W_CKV], m_sc[...], l_sc[...], acc_sc[...])
        m_sc[...], l_sc[...], acc_sc[...] = m, l, acc

    @pl.when(ki == pl.num_programs(2) - 1)
    def _():
        o = acc_sc[...] / l_sc[...]
        o_ref[...] = o.reshape(heads, tq, W_CKV).astype(o_ref.dtype)


def mla_attn_prompt(q, kc, batch, *, tq=256, tk=512):
    heads, M, _ = q.shape
    T = M // batch
    tq, tk = _pick(T, tq), _pick(T, tk)
    nq, nk = T // tq, T // tk
    kmap = lambda b, qi, ki: (b * nk + jnp.minimum(ki, ((qi + 1) * tq - 1) // tk), 0)
    return pl.pallas_call(
        functools.partial(_mla_prompt_kernel, heads=heads, tq=tq, tk=tk), name="mla_attn_prompt",
        grid=(batch, nq, nk),
        in_specs=[pl.BlockSpec((heads, tq, QK_W), lambda b, qi, ki: (0, b * nq + qi, 0)),
                  pl.BlockSpec((tk, QK_W), kmap)],
        out_specs=pl.BlockSpec((heads, tq, W_CKV), lambda b, qi, ki: (0, b * nq + qi, 0)),
        out_shape=jax.ShapeDtypeStruct((heads, M, W_CKV), BF16),
        scratch_shapes=[pltpu.VMEM((heads * tq, 1), F32), pltpu.VMEM((heads * tq, 1), F32),
                        pltpu.VMEM((heads * tq, W_CKV), F32)],
        compiler_params=_cparams(("parallel", "parallel", "arbitrary")),
    )(q, kc)


def _page_copies(pt_ref, hbm_refs, bufs, sems, layer, b, chunk, slot, pages_per_chunk):
    copies = []
    for i in range(pages_per_chunk):
        page = pt_ref[b, chunk * pages_per_chunk + i]
        for n, (hbm, buf) in enumerate(zip(hbm_refs, bufs)):
            copies.append(pltpu.make_async_copy(hbm.at[layer, page], buf.at[slot, i], sems.at[n, slot]))
    return copies


def _mla_sample_kernel(pt_ref, q_ref, knew_ref, lat_hbm, rope_hbm, o_ref, latbuf, ropebuf, sems, knew_pad,
                       *, layer, heads, t_new, pages_per_chunk, n_chunks, page):
    b = pl.program_id(0)
    nb = pl.num_programs(0)
    rows = heads * t_new
    hbm_refs, bufs = (lat_hbm, rope_hbm), (latbuf, ropebuf)
    copies = functools.partial(_page_copies, pt_ref, hbm_refs, bufs, sems, layer,
                               pages_per_chunk=pages_per_chunk)

    @pl.when(b == 0)
    def _():
        for c in copies(0, 0, 0):
            c.start()

    q = q_ref[0]
    q_lat, q_rope = q[:, 0:W_CKV], q[:, W_CKV:W_CKV + MLA_ROPE]
    keys = pages_per_chunk * page

    def chunk_step(c, carry):
        g = b * n_chunks + c
        slot = lax.rem(g, 2)
        last_chunk = c == n_chunks - 1
        nb_, nc_ = jnp.where(last_chunk, b + 1, b), jnp.where(last_chunk, 0, c + 1)

        @pl.when(g + 1 < nb * n_chunks)
        def _():
            for cp in copies(nb_, nc_, 1 - slot):
                cp.start()

        for cp in copies(b, c, slot):
            cp.wait()
        lat = latbuf[slot].reshape(keys, W_CKV).astype(BF16)
        rp = ropebuf[slot].reshape(keys, MLA_ROPE).astype(BF16)
        s = _dot_nt(q_lat, lat) + _dot_nt(q_rope, rp)
        return _softmax_step(s, lat, *carry)

    init = (jnp.full((rows, 1), NEG_BIG, F32), jnp.zeros((rows, 1), F32), jnp.zeros((rows, W_CKV), F32))
    m, l, acc = lax.fori_loop(0, n_chunks, chunk_step, init)

    knew_pad[...] = jnp.zeros_like(knew_pad)
    knew_pad[0:t_new, :] = knew_ref[0]
    kn = knew_pad[...]
    s = _dot_nt(q, kn)
    t_of_row = lax.broadcasted_iota(jnp.int32, s.shape, 0) % t_new
    j = lax.broadcasted_iota(jnp.int32, s.shape, 1)
    s = jnp.where((j <= t_of_row) & (j < t_new), s, NEG_BIG)
    m, l, acc = _softmax_step(s, kn[:, 0:W_CKV], m, l, acc)
    o_ref[0] = (acc / l).astype(o_ref.dtype)


def mla_attn_sample(page_table, q, knew, cache_lat, cache_rope, layer, *, pages_per_chunk=16):
    bs, rows, _ = q.shape
    t_new = knew.shape[1]
    heads = rows // t_new
    n_pages = page_table.shape[1]
    page = cache_lat.shape[2]
    ppc = _pick(n_pages, pages_per_chunk)
    kern = functools.partial(_mla_sample_kernel, layer=layer, heads=heads, t_new=t_new,
                             pages_per_chunk=ppc, n_chunks=n_pages // ppc, page=page)
    grid_spec = pltpu.PrefetchScalarGridSpec(
        num_scalar_prefetch=1,
        grid=(bs,),
        in_specs=[
            pl.BlockSpec((1, rows, QK_W), lambda b, pt: (b, 0, 0)),
            pl.BlockSpec((1, t_new, QK_W), lambda b, pt: (b, 0, 0)),
            pl.BlockSpec(memory_space=pl.ANY),
            pl.BlockSpec(memory_space=pl.ANY),
        ],
        out_specs=pl.BlockSpec((1, rows, W_CKV), lambda b, pt: (b, 0, 0)),
        scratch_shapes=[
            pltpu.VMEM((2, ppc, page, W_CKV), F32),
            pltpu.VMEM((2, ppc, page, MLA_ROPE), F32),
            pltpu.SemaphoreType.DMA((2, 2)),
            pltpu.VMEM((LANES, QK_W), BF16),
        ],
    )
    return pl.pallas_call(
        kern, grid_spec=grid_spec, name="mla_attn_sample",
        out_shape=jax.ShapeDtypeStruct((bs, rows, W_CKV), BF16),
        compiler_params=_cparams(("arbitrary",)),
    )(page_table, q, knew, cache_lat, cache_rope)


def _mla_post_kernel(o_ref, uv_ref, g_ref, out_ref, *, heads):
    for h in range(heads):
        y = _dot(o_ref[h], uv_ref[h])
        out_ref[:, h * MLA_V:(h + 1) * MLA_V] = _rms(y, g_ref[h:h + 1, :]).astype(out_ref.dtype)


def mla_post(o_lat, uv, g_out, *, tm=512):
    heads, M, _ = o_lat.shape
    tm = _pick(M, tm)
    return pl.pallas_call(
        functools.partial(_mla_post_kernel, heads=heads), name="mla_post",
        grid=(M // tm,),
        in_specs=[pl.BlockSpec((heads, tm, W_CKV), lambda i: (0, i, 0)),
                  pl.BlockSpec(uv.shape, lambda i: (0, 0, 0)),
                  pl.BlockSpec(g_out.shape, lambda i: (0, 0))],
        out_specs=pl.BlockSpec((tm, heads * MLA_V), lambda i: (i, 0)),
        out_shape=jax.ShapeDtypeStruct((M, heads * MLA_V), BF16),
        compiler_params=_cparams(("parallel",)),
    )(o_lat, uv, g_out)


def _suffix_matrix(n):
    r = lax.broadcasted_iota(jnp.int32, (n, n), 0)
    c = lax.broadcasted_iota(jnp.int32, (n, n), 1)
    return jnp.where(r > c, 1.0, 0.0).astype(BF16)


def _suffix_sums(lk, u):
    hi = lk.astype(BF16)
    mid = (lk - hi.astype(F32)).astype(BF16)
    return _dot(hi, u) + _dot(mid, u)


def _sb_block(q, k, v, mask, u, carry, acc):
    z = _dot_nt(q, k)
    lp = _log_sigmoid(z)
    lk = lp - z
    if mask is not None:
        lk = jnp.where(mask, lk, 0.0)
    la = _suffix_sums(lk, u)
    a = jnp.exp(lp + la + carry)
    if mask is not None:
        a = jnp.where(mask, a, 0.0)
    acc = acc + _dot(a.astype(BF16), v)
    carry = carry + la[:, 0:1] + lk[:, 0:1]
    return carry, acc


def _sb_prompt_kernel(q_ref, k_ref, v_ref, g_ref, o_ref, qs_sc, carry_sc, acc_sc, *, heads, tq, tk, scale):
    qi, ki = pl.program_id(1), pl.program_id(2)
    last = ((qi + 1) * tq - 1) // tk
    rows = heads * tq

    @pl.when(ki == 0)
    def _():
        q = q_ref[...]
        for h in range(heads):
            qs_sc[h * tq:(h + 1) * tq, :] = (q[:, h * SB_D:(h + 1) * SB_D] * scale).astype(BF16)
        carry_sc[...] = jnp.zeros_like(carry_sc)
        acc_sc[...] = jnp.zeros_like(acc_sc)

    @pl.when(ki <= last)
    def _():
        kb = last - ki
        q_pos = qi * tq + (lax.broadcasted_iota(jnp.int32, (rows, tk), 0) & (tq - 1))
        k_pos = kb * tk + lax.broadcasted_iota(jnp.int32, (rows, tk), 1)
        carry, acc = _sb_block(qs_sc[...], k_ref[...].astype(BF16), v_ref[...].astype(BF16), k_pos < q_pos,
                               _suffix_matrix(tk), carry_sc[...], acc_sc[...])
        carry_sc[...], acc_sc[...] = carry, acc

    @pl.when(ki == pl.num_programs(2) - 1)
    def _():
        for h in range(heads):
            o = _rms(acc_sc[h * tq:(h + 1) * tq, :], g_ref[h:h + 1, :])
            o_ref[:, h * SB_D:(h + 1) * SB_D] = o.astype(o_ref.dtype)


def sb_attn_prompt(proj, g_out, batch, *, tq=128, tk=256):
    M = proj.shape[0]
    heads = g_out.shape[0]
    T = M // batch
    tq, tk = _pick(T, tq), _pick(T, tk)
    assert tq & (tq - 1) == 0
    nq, nk = T // tq, T // tk

    def kvmap(colblock):
        def f(b, qi, ki):
            last = ((qi + 1) * tq - 1) // tk
            return (b * nk + jnp.maximum(last - ki, 0), colblock)
        return f

    return pl.pallas_call(
        functools.partial(_sb_prompt_kernel, heads=heads, tq=tq, tk=tk, scale=float(SB_D ** -0.5)),
        name="sb_attn_prompt",
        grid=(batch, nq, nk),
        in_specs=[pl.BlockSpec((tq, W_SQ), lambda b, qi, ki: (b * nq + qi, OFF_SQ // W_SQ)),
                  pl.BlockSpec((tk, SB_D), kvmap(OFF_SK // SB_D)),
                  pl.BlockSpec((tk, SB_D), kvmap(OFF_SV // SB_D)),
                  pl.BlockSpec(g_out.shape, lambda b, qi, ki: (0, 0))],
        out_specs=pl.BlockSpec((tq, heads * SB_D), lambda b, qi, ki: (b * nq + qi, 0)),
        out_shape=jax.ShapeDtypeStruct((M, heads * SB_D), BF16),
        scratch_shapes=[pltpu.VMEM((heads * tq, SB_D), BF16), pltpu.VMEM((heads * tq, 1), F32),
                        pltpu.VMEM((heads * tq, SB_D), F32)],
        compiler_params=_cparams(("parallel", "parallel", "arbitrary")),
    )(proj, proj, proj, g_out)


def _sb_sample_kernel(pt_ref, q_ref, knew_ref, vnew_ref, g_ref, k_hbm, v_hbm, o_ref, kbuf, vbuf, sems,
                      knew_pad, vnew_pad, *, layer, t_new, pages_per_chunk, n_chunks, page, sub, scale):
    b = pl.program_id(0)
    nb = pl.num_programs(0)
    rows = q_ref.shape[1]
    copies = functools.partial(_page_copies, pt_ref, (k_hbm, v_hbm), (kbuf, vbuf), sems, layer,
                               pages_per_chunk=pages_per_chunk)
    chunk_of = lambda c: n_chunks - 1 - c

    @pl.when(b == 0)
    def _():
        for cp in copies(0, chunk_of(0), 0):
            cp.start()

    q = (q_ref[0] * scale).astype(BF16)

    knew_pad[...] = jnp.zeros_like(knew_pad)
    vnew_pad[...] = jnp.zeros_like(vnew_pad)
    knew_pad[0:t_new, :] = knew_ref[0]
    vnew_pad[0:t_new, :] = vnew_ref[0]
    t_of_row = lax.broadcasted_iota(jnp.int32, (rows, LANES), 0) % t_new
    j = lax.broadcasted_iota(jnp.int32, (rows, LANES), 1)
    carry, acc = _sb_block(q, knew_pad[...].astype(BF16), vnew_pad[...].astype(BF16), j < t_of_row,
                           _suffix_matrix(LANES), jnp.zeros((rows, 1), F32), jnp.zeros((rows, SB_D), F32))

    keys = sub * page
    u = _suffix_matrix(keys)

    def chunk_step(c, state):
        g = b * n_chunks + c
        slot = lax.rem(g, 2)
        last_chunk = c == n_chunks - 1
        nb_, nc_ = jnp.where(last_chunk, b + 1, b), jnp.where(last_chunk, 0, c + 1)

        @pl.when(g + 1 < nb * n_chunks)
        def _():
            for cp in copies(nb_, chunk_of(nc_), 1 - slot):
                cp.start()

        for cp in copies(b, chunk_of(c), slot):
            cp.wait()
        carry, acc = state
        for sb in reversed(range(pages_per_chunk // sub)):
            k = kbuf[slot, sb * sub:(sb + 1) * sub].reshape(keys, SB_D).astype(BF16)
            v = vbuf[slot, sb * sub:(sb + 1) * sub].reshape(keys, SB_D).astype(BF16)
            carry, acc = _sb_block(q, k, v, None, u, carry, acc)
        return carry, acc

    carry, acc = lax.fori_loop(0, n_chunks, chunk_step, (carry, acc))
    o_ref[0] = _rms(acc, g_ref[...]).astype(o_ref.dtype)


def sb_attn_sample(page_table, q, knew, vnew, g_rows, cache_k, cache_v, layer, *, pages_per_chunk=16, sub=2):
    bs, rows, _ = q.shape
    t_new = knew.shape[1]
    n_pages = page_table.shape[1]
    page = cache_k.shape[2]
    ppc = _pick(n_pages, pages_per_chunk)
    sub = _pick(ppc, sub)
    kern = functools.partial(_sb_sample_kernel, layer=layer, t_new=t_new, pages_per_chunk=ppc,
                             n_chunks=n_pages // ppc, page=page, sub=sub, scale=float(SB_D ** -0.5))
    grid_spec = pltpu.PrefetchScalarGridSpec(
        num_scalar_prefetch=1,
        grid=(bs,),
        in_specs=[
            pl.BlockSpec((1, rows, SB_D), lambda b, pt: (b, 0, 0)),
            pl.BlockSpec((1, t_new, SB_D), lambda b, pt: (b, 0, 0)),
            pl.BlockSpec((1, t_new, SB_D), lambda b, pt: (b, 0, 0)),
            pl.BlockSpec((rows, SB_D), lambda b, pt: (0, 0)),
            pl.BlockSpec(memory_space=pl.ANY),
            pl.BlockSpec(memory_space=pl.ANY),
        ],
        out_specs=pl.BlockSpec((1, rows, SB_D), lambda b, pt: (b, 0, 0)),
        scratch_shapes=[
            pltpu.VMEM((2, ppc, page, SB_D), F32),
            pltpu.VMEM((2, ppc, page, SB_D), F32),
            pltpu.SemaphoreType.DMA((2, 2)),
            pltpu.VMEM((LANES, SB_D), F32),
            pltpu.VMEM((LANES, SB_D), F32),
        ],
    )
    return pl.pallas_call(
        kern, grid_spec=grid_spec, name="sb_attn_sample",
        out_shape=jax.ShapeDtypeStruct((bs, rows, SB_D), BF16),
        compiler_params=_cparams(("arbitrary",)),
    )(page_table, q, knew, vnew, g_rows, cache_k, cache_v)


def _dn_kernel(qkv_ref, z_ref, ab_ref, prev_ref, s0_ref, wconv_ref, alog_ref, dtb_ref, gout_ref,
               o_ref, snew_ref, xe_sc, s_sc, abp_sc, zp_sc, *, heads, chunk, t_blk):
    n = pl.program_id(1)
    C = chunk
    hk = heads * DN_DK
    padded = t_blk < C

    @pl.when(n == 0)
    def _():
        xe_sc[0:SUBLANES, :] = jnp.zeros((SUBLANES, xe_sc.shape[1]), F32)
        xe_sc[SUBLANES - (DN_CONV - 1):SUBLANES, :] = prev_ref[0]
        s_sc[...] = s0_ref[0]

    if padded:
        xe_sc[SUBLANES:SUBLANES + C, :] = jnp.zeros((C, xe_sc.shape[1]), F32)
        abp_sc[...] = jnp.zeros_like(abp_sc)
        zp_sc[...] = jnp.zeros_like(zp_sc)
        abp_sc[0:t_blk, :] = ab_ref[0]
        zp_sc[0:t_blk, :] = z_ref[0]
        ab, zg = abp_sc[...], zp_sc[...]
    else:
        ab, zg = ab_ref[0], z_ref[0]
    xe_sc[SUBLANES:SUBLANES + t_blk, :] = qkv_ref[0]

    w = wconv_ref[...]
    conv = w[DN_CONV - 1:DN_CONV, :] * xe_sc[SUBLANES:SUBLANES + C, :]
    for i in range(1, DN_CONV):
        conv = conv + w[DN_CONV - 1 - i:DN_CONV - i, :] * xe_sc[SUBLANES - i:SUBLANES - i + C, :]
    if not padded:
        xe_sc[0:SUBLANES, :] = xe_sc[C:C + SUBLANES, :]
    act = conv * jax.nn.sigmoid(conv)

    row = lax.broadcasted_iota(jnp.int32, (C, LANES), 0)
    lane = lax.broadcasted_iota(jnp.int32, (C, LANES), 1)
    valid = row < t_blk
    g4 = -jnp.exp(alog_ref[...]) * _softplus(ab + dtb_ref[...])
    g4 = jnp.where(valid & (lane < heads), g4, 0.0)
    beta4 = jnp.where(valid, jax.nn.sigmoid(ab), 0.0)

    r = lax.broadcasted_iota(jnp.int32, (C, C), 0)
    c = lax.broadcasted_iota(jnp.int32, (C, C), 1)
    incl, strict = r >= c, r > c
    lower = jnp.where(incl, 1.0, 0.0)
    eye = jnp.where(r == c, 1.0, 0.0)
    g_cum = _dotx(lower, g4)

    for h in range(heads):
        qh = act[:, h * DN_DK:(h + 1) * DN_DK]
        kh = act[:, hk + h * DN_DK:hk + (h + 1) * DN_DK]
        vh = act[:, 2 * hk + h * DN_DV:2 * hk + (h + 1) * DN_DV]
        qh = qh * lax.rsqrt(jnp.sum(qh * qh, axis=-1, keepdims=True) + L2_EPS) * (DN_DK ** -0.5)
        kh = kh * lax.rsqrt(jnp.sum(kh * kh, axis=-1, keepdims=True) + L2_EPS)
        if padded:
            qh = jnp.where(valid, qh, 0.0)
            kh = jnp.where(valid, kh, 0.0)
            vh = jnp.where(valid, vh, 0.0)
        g_col = g4[:, h:h + 1]
        gc_col = g_cum[:, h:h + 1]
        beta = beta4[:, heads + h:heads + h + 1]

        diff = _dotx(lower, g_col * jnp.where(strict, 1.0, 0.0))
        decay = jnp.where(incl, jnp.exp(diff), 0.0)
        a = jnp.where(strict, beta * _dotx_nt(kh, kh) * decay, 0.0)
        tinv, p = eye - a, a
        for _ in range(int(math.log2(C)) - 1):
            p = _dotx(p, p)
            tinv = tinv + _dotx(tinv, p)
        e_g = jnp.exp(gc_col)
        u = _dotx(tinv, beta * vh)
        wmat = _dotx(tinv, (beta * e_g) * kh)
        qk = _dotx_nt(qh, kh) * decay
        g_last = gc_col[C - 1:C, :]
        q_dec = qh * e_g
        k_dec = kh * jnp.exp(g_last - gc_col)

        s = s_sc[h]
        delta = u - _dotx(wmat, s)
        o = _dotx(q_dec, s) + _dotx(qk, delta)
        s_sc[h] = jnp.exp(g_last) * s + _dotx_tn(k_dec, delta)

        zh = zg[:, h * DN_DV:(h + 1) * DN_DV]
        out = _rms(o, gout_ref[...]) * (zh * jax.nn.sigmoid(zh))
        o_ref[0, :, h * DN_DV:(h + 1) * DN_DV] = out[0:t_blk].astype(o_ref.dtype)

    @pl.when(n == pl.num_programs(1) - 1)
    def _():
        snew_ref[0] = s_sc[...]


def deltanet(proj3, prev_conv, s0, w_conv, a_log, dt_bias, g_out, *, n_chunks, chunk):
    groups, t_blk, _ = proj3.shape
    batch = groups // n_chunks
    heads = s0.shape[1]
    qkv_w = w_conv.shape[1]
    assert qkv_w == W_QKV and heads * DN_DV == W_Z
    pad = lambda v: jnp.zeros((1, LANES), F32).at[0, :heads].set(v)
    gmap = lambda colblock: (lambda b, n: (b * n_chunks + n, 0, colblock))
    return pl.pallas_call(
        functools.partial(_dn_kernel, heads=heads, chunk=chunk, t_blk=t_blk), name="deltanet",
        grid=(batch, n_chunks),
        in_specs=[
            pl.BlockSpec((1, t_blk, W_QKV), gmap(OFF_QKV // W_QKV)),
            pl.BlockSpec((1, t_blk, W_Z), gmap(OFF_Z // W_Z)),
            pl.BlockSpec((1, t_blk, LANES), gmap(OFF_AB // LANES)),
            pl.BlockSpec((1, DN_CONV - 1, W_QKV), lambda b, n: (b, 0, 0)),
            pl.BlockSpec((1, heads, DN_DK, DN_DV), lambda b, n: (b, 0, 0, 0)),
            pl.BlockSpec((DN_CONV, W_QKV), lambda b, n: (0, 0)),
            pl.BlockSpec((1, LANES), lambda b, n: (0, 0)),
            pl.BlockSpec((1, LANES), lambda b, n: (0, 0)),
            pl.BlockSpec((1, DN_DV), lambda b, n: (0, 0)),
        ],
        out_specs=[
            pl.BlockSpec((1, t_blk, W_Z), lambda b, n: (b * n_chunks + n, 0, 0)),
            pl.BlockSpec((1, heads, DN_DK, DN_DV), lambda b, n: (b, 0, 0, 0)),
        ],
        out_shape=[
            jax.ShapeDtypeStruct((groups, t_blk, W_Z), BF16),
            jax.ShapeDtypeStruct((batch, heads, DN_DK, DN_DV), F32),
        ],
        scratch_shapes=[
            pltpu.VMEM((chunk + SUBLANES, W_QKV), F32),
            pltpu.VMEM((heads, DN_DK, DN_DV), F32),
            pltpu.VMEM((chunk, LANES), F32),
            pltpu.VMEM((chunk, W_Z), F32),
        ],
        compiler_params=_cparams(("parallel", "arbitrary")),
    )(proj3, proj3, proj3, prev_conv, s0, w_conv, pad(a_log), pad(dt_bias), g_out.reshape(1, -1))


def _pack_w_in(w, dims):
    ql, kvl, dnh, sbh = dims
    sizes = (ql, kvl, MLA_ROPE, 3 * dnh * DN_DK, dnh * DN_DV, dnh, dnh, sbh * SB_D, SB_D, SB_D)
    parts, off = [], 0
    for n in sizes:
        parts.append(w[:, off:off + n])
        off += n
    cq, ckv, kr, qkv, z, a, b, sq, sk, sv = parts
    half = MLA_ROPE // 2
    kr_rot = jnp.concatenate([-kr[:, half:], kr[:, :half]], axis=1)
    zeros = lambda n: jnp.zeros((w.shape[0], n), w.dtype)
    cols = [qkv, cq, sk, z, sq, ckv, sv, kr, zeros(LANES - MLA_ROPE), kr_rot, zeros(LANES - MLA_ROPE),
            a, b, zeros(LANES - 2 * dnh)]
    packed = jnp.concatenate(cols, axis=1)
    assert packed.shape[1] == NP_IN, packed.shape
    return packed.astype(BF16)


def _pack_uq(uq):
    ql, heads, _ = uq.shape
    half = MLA_ROPE // 2
    nope = uq[:, :, :MLA_NOPE].reshape(ql, heads * MLA_NOPE)
    rope = uq[:, :, MLA_NOPE:]
    rot = jnp.concatenate([-rope[:, :, half:], rope[:, :, :half]], axis=2)
    padz = jnp.zeros((ql, heads, LANES - MLA_ROPE), uq.dtype)
    rope_p = jnp.concatenate([rope, padz], axis=2).reshape(ql, heads * LANES)
    rot_p = jnp.concatenate([rot, padz], axis=2).reshape(ql, heads * LANES)
    return jnp.concatenate([nope, rope_p, rot_p], axis=1).astype(BF16)


def _rope_tables(pos):
    half = MLA_ROPE // 2
    inv_freq = ROPE_THETA ** (-jnp.arange(half, dtype=F32) / half)
    ang = pos.astype(F32)[:, None] * inv_freq[None, :]
    z = jnp.zeros((pos.shape[0], LANES - MLA_ROPE), F32)
    cos = jnp.concatenate([jnp.cos(ang), jnp.cos(ang), z], axis=1)
    sin = jnp.concatenate([jnp.sin(ang), jnp.sin(ang), z], axis=1)
    return cos, sin


def _layer(x, h, lw, group):
    M = x.shape[0]
    batch, t_len = group["batch"], group["t"]
    proj = matmul(h, lw["w_in"], F32, tm=1024, tn=1280)
    q, kc, ckv_o, kr_o = mla_prep(proj, group["cos"], group["sin"], lw["g_mla_q"], lw["g_mla_kv"],
                                  lw["uq"], lw["ukt"])
    heads = q.shape[0]
    sbh = lw["g_sb_out"].shape[0]
    if group["paged"] is None:
        o_lat = mla_attn_prompt(q, kc, batch)
        o_sb = sb_attn_prompt(proj, lw["g_sb_out"], batch)
        n_chunks, chunk = t_len // DN_CHUNK, DN_CHUNK
        prev_conv = jnp.zeros((batch, DN_CONV - 1, W_QKV), F32)
        s0 = jnp.zeros((batch, W_Z // DN_DV, DN_DK, DN_DV), F32)
    else:
        pg = group["paged"]
        layer = pg["layer"]
        to_rows = lambda a, nh: a.reshape(nh, batch, t_len, a.shape[-1]).transpose(1, 0, 2, 3).reshape(
            batch, nh * t_len, a.shape[-1])
        from_rows = lambda a, nh: a.reshape(batch, nh, t_len, a.shape[-1]).transpose(1, 0, 2, 3).reshape(
            nh, M, a.shape[-1])
        o_rows = mla_attn_sample(pg["page_table"], to_rows(q, heads), kc.reshape(batch, t_len, QK_W),
                                 pg["lat"], pg["rope"], layer)
        o_lat = from_rows(o_rows, heads)
        sq = proj[:, OFF_SQ:OFF_SQ + W_SQ].reshape(batch, t_len, sbh, SB_D).transpose(0, 2, 1, 3)
        sk = proj[:, OFF_SK:OFF_SK + SB_D].reshape(batch, t_len, SB_D)
        sv = proj[:, OFF_SV:OFF_SV + SB_D].reshape(batch, t_len, SB_D)
        g_rows = jnp.repeat(lw["g_sb_out"], t_len, axis=0)
        o_sb_rows = sb_attn_sample(pg["page_table"], sq.reshape(batch, sbh * t_len, SB_D), sk, sv, g_rows,
                                   pg["sbk"], pg["sbv"], layer)
        o_sb = o_sb_rows.reshape(batch, sbh, t_len, SB_D).transpose(0, 2, 1, 3).reshape(M, sbh * SB_D)
        n_chunks, chunk = 1, SUBLANES
        assert t_len <= chunk
        prev_conv, s0 = pg["conv"], pg["S"]
    o_mla = mla_post(o_lat, lw["uv"], lw["g_mla_out"])
    t_blk = t_len // n_chunks
    o_dn, s_new = deltanet(proj.reshape(batch * n_chunks, t_blk, NP_IN), prev_conv, s0, lw["w_conv"],
                           lw["a_log"], lw["dt_bias"], lw["g_dn_out"], n_chunks=n_chunks, chunk=chunk)
    mixed = jnp.concatenate([o_mla, o_dn.reshape(M, W_Z), o_sb], axis=1)
    x1, hm = matmul_res_norm(mixed, lw["w_out"], x, lw["g_post_mix"], lw["g_pre_mlp"])
    ff = matmul(hm, lw["w_up"], BF16, relu2=True, tm=1024, tn=1024)
    x2, h_next = matmul_res_norm(ff, lw["w_down"], x1, lw["g_post_mlp"], lw["g_next"])
    proj_b = proj.reshape(batch, t_len, NP_IN)
    state = (ckv_o.reshape(batch, t_len, W_CKV), kr_o.reshape(batch, t_len, MLA_ROPE),
             proj_b[:, :, OFF_SK:OFF_SK + SB_D], proj_b[:, :, OFF_SV:OFF_SV + SB_D], s_new,
             proj_b[:, t_len - (DN_CONV - 1):, OFF_QKV:OFF_QKV + W_QKV])
    return x2, h_next, state


def kernel(x_prompt, x_sample, cache_mla_latent, cache_mla_rope, cache_sb_k, cache_sb_v, state_dn_S, state_dn_conv, page_table, w_in, g_pre_mix, g_mla_q, w_mla_uq, g_mla_kv, w_mla_uk, w_mla_uv, g_mla_out, w_dn_conv, dn_A_log, dn_dt_bias, g_dn_out, g_sb_out, w_out, g_post_mix, g_pre_mlp, w_up, w_down, g_post_mlp):
    depth = w_in.shape[0]
    B, T, D = x_prompt.shape
    Bs, Ts, _ = x_sample.shape
    past_len = page_table.shape[1] * cache_mla_latent.shape[2]
    dims = (w_mla_uq.shape[1], w_mla_uk.shape[1], dn_A_log.shape[1], g_sb_out.shape[1])
    assert dims == (W_CQ, W_CKV, W_Z // DN_DV, W_SQ // SB_D) and T % DN_CHUNK == 0

    cos_p, sin_p = _rope_tables(jnp.tile(jnp.arange(T), B))
    cos_s, sin_s = _rope_tables(jnp.tile(past_len + jnp.arange(Ts), Bs))
    groups = [dict(batch=B, t=T, cos=cos_p, sin=sin_p, paged=None),
              dict(batch=Bs, t=Ts, cos=cos_s, sin=sin_s, paged=None)]
    xs = [x_prompt.reshape(B * T, D), x_sample.reshape(Bs * Ts, D)]
    hs = [norm_cast(x, g_pre_mix[0]) for x in xs]
    states = [[], []]
    for l in range(depth):
        lw = dict(
            w_in=_pack_w_in(w_in[l], dims), g_mla_q=g_mla_q[l], g_mla_kv=g_mla_kv[l],
            uq=_pack_uq(w_mla_uq[l]), ukt=w_mla_uk[l].transpose(1, 2, 0).astype(BF16),
            uv=w_mla_uv[l].transpose(1, 0, 2).astype(BF16), g_mla_out=g_mla_out[l],
            w_conv=w_dn_conv[l], a_log=dn_A_log[l], dt_bias=dn_dt_bias[l], g_dn_out=g_dn_out[l],
            g_sb_out=g_sb_out[l], w_out=w_out[l].astype(BF16), g_post_mix=g_post_mix[l],
            g_pre_mlp=g_pre_mlp[l], w_up=w_up[l].astype(BF16), w_down=w_down[l].astype(BF16),
            g_post_mlp=g_post_mlp[l], g_next=g_pre_mix[(l + 1) % depth])
        groups[1]["paged"] = dict(layer=l, page_table=page_table, lat=cache_mla_latent, rope=cache_mla_rope,
                                  sbk=cache_sb_k, sbv=cache_sb_v, S=state_dn_S[l], conv=state_dn_conv[l])
        for gi in range(2):
            xs[gi], hs[gi], st = _layer(xs[gi], hs[gi], lw, groups[gi])
            states[gi].append(st)
    p_state = tuple(jnp.stack(t) for t in zip(*states[0]))
    s_state = tuple(jnp.stack(t) for t in zip(*states[1]))
    return (xs[0].reshape(B, T, D), xs[1].reshape(Bs, Ts, D)) + p_state + s_state
```

```python
import functools
import math

import jax
import jax.numpy as jnp
from jax import lax
from jax.experimental import pallas as pl
from jax.experimental.pallas import tpu as pltpu

F32 = jnp.float32
BF16 = jnp.bfloat16

MLA_NOPE = 128
MLA_ROPE = 64
MLA_V = 128
DN_DK = 128
DN_DV = 128
DN_CONV = 4
DN_CHUNK = 64
SB_D = 128
NORM_EPS = 1e-6
L2_EPS = 1e-6
NEG_BIG = -1e30
ROPE_THETA = 10000.0

LANES = 128
SUBLANES = 8
VMEM_LIMIT = 48 * 1024 * 1024

OFF_QKV, W_QKV = 0, 1536
OFF_CQ, W_CQ = 1536, 384
OFF_SK = 1920
OFF_Z, W_Z = 2048, 512
OFF_SQ, W_SQ = 2560, 512
OFF_CKV, W_CKV = 3072, 256
OFF_SV = 3328
OFF_KR = 3456
OFF_KRR = 3584
OFF_AB = 3712
NP_IN = 3840
QK_W = 384

NN_DIMS = (((1,), (0,)), ((), ()))
NT_DIMS = (((1,), (1,)), ((), ()))
TN_DIMS = (((0,), (0,)), ((), ()))


def _cparams(sem):
    return pltpu.CompilerParams(dimension_semantics=sem, vmem_limit_bytes=VMEM_LIMIT)


def _rms(x, g):
    return x * lax.rsqrt(jnp.mean(x * x, axis=-1, keepdims=True) + NORM_EPS) * g


def _dot(a, b):
    return jnp.dot(a, b, preferred_element_type=F32)


def _dot_nt(a, b):
    return lax.dot_general(a, b, NT_DIMS, preferred_element_type=F32)


def _log_sigmoid(z):
    return jnp.minimum(z, 0.0) - jnp.log1p(jnp.exp(-jnp.abs(z)))


def _softplus(x):
    return jnp.maximum(x, 0.0) + jnp.log1p(jnp.exp(-jnp.abs(x)))


def _pick(total, want):
    t = min(total, want)
    assert total % t == 0, (total, want)
    return t


def _norm_cast_kernel(x_ref, g_ref, o_ref):
    o_ref[...] = _rms(x_ref[...], g_ref[...]).astype(o_ref.dtype)


def norm_cast(x, g):
    M, D = x.shape
    tm = _pick(M, 512)
    return pl.pallas_call(
        _norm_cast_kernel, name="norm_cast",
        grid=(M // tm,),
        in_specs=[pl.BlockSpec((tm, D), lambda i: (i, 0)), pl.BlockSpec((1, D), lambda i: (0, 0))],
        out_specs=pl.BlockSpec((tm, D), lambda i: (i, 0)),
        out_shape=jax.ShapeDtypeStruct((M, D), BF16),
        compiler_params=_cparams(("parallel",)),
    )(x, g.reshape(1, D))


def _mm_kernel(a_ref, w_ref, o_ref, *, relu2):
    acc = _dot(a_ref[...], w_ref[...])
    if relu2:
        acc = jnp.square(jnp.maximum(acc, 0.0))
    o_ref[...] = acc.astype(o_ref.dtype)


def matmul(a, w, out_dtype, *, relu2=False, tm=1024, tn=1024):
    M, K = a.shape
    N = w.shape[1]
    tm, tn = _pick(M, tm), _pick(N, tn)
    return pl.pallas_call(
        functools.partial(_mm_kernel, relu2=relu2), name="matmul_relu2" if relu2 else "matmul",
        grid=(M // tm, N // tn),
        in_specs=[pl.BlockSpec((tm, K), lambda i, j: (i, 0)), pl.BlockSpec((K, tn), lambda i, j: (0, j))],
        out_specs=pl.BlockSpec((tm, tn), lambda i, j: (i, j)),
        out_shape=jax.ShapeDtypeStruct((M, N), out_dtype),
        compiler_params=_cparams(("parallel", "arbitrary")),
    )(a, w)


def _mm_res_norm_kernel(a_ref, w_ref, x_ref, gp_ref, gn_ref, xo_ref, ho_ref, acc_ref):
    k = pl.program_id(1)

    @pl.when(k == 0)
    def _():
        acc_ref[...] = jnp.zeros_like(acc_ref)

    acc_ref[...] += _dot(a_ref[...], w_ref[...])

    @pl.when(k == pl.num_programs(1) - 1)
    def _():
        xn = x_ref[...] + _rms(acc_ref[...], gp_ref[...])
        xo_ref[...] = xn
        ho_ref[...] = _rms(xn, gn_ref[...]).astype(ho_ref.dtype)


def matmul_res_norm(a, w, x, g_post, g_next, *, tm=512, tk=1024):
    M, K = a.shape
    N = w.shape[1]
    tm, tk = _pick(M, tm), _pick(K, tk)
    return pl.pallas_call(
        _mm_res_norm_kernel, name="matmul_res_norm",
        grid=(M // tm, K // tk),
        in_specs=[
            pl.BlockSpec((tm, tk), lambda i, k: (i, k)),
            pl.BlockSpec((tk, N), lambda i, k: (k, 0)),
            pl.BlockSpec((tm, N), lambda i, k: (i, 0)),
            pl.BlockSpec((1, N), lambda i, k: (0, 0)),
            pl.BlockSpec((1, N), lambda i, k: (0, 0)),
        ],
        out_specs=[pl.BlockSpec((tm, N), lambda i, k: (i, 0)), pl.BlockSpec((tm, N), lambda i, k: (i, 0))],
        out_shape=[jax.ShapeDtypeStruct((M, N), F32), jax.ShapeDtypeStruct((M, N), BF16)],
        scratch_shapes=[pltpu.VMEM((tm, N), F32)],
        compiler_params=_cparams(("parallel", "arbitrary")),
    )(a, w, x, g_post.reshape(1, N), g_next.reshape(1, N))


def _mla_prep_kernel(cq_ref, ckv_ref, kr_ref, krr_ref, cos_ref, sin_ref, gq_ref, gkv_ref, uq_ref, ukt_ref,
                     q_ref, kc_ref, ckvo_ref, kro_ref, *, heads, scale):
    cqn = _rms(cq_ref[...], gq_ref[...]).astype(BF16)
    q = _dot(cqn, uq_ref[...])
    cos, sin = cos_ref[...], sin_ref[...]
    hw = heads * LANES
    for h in range(heads):
        lo, hi = h * LANES, (h + 1) * LANES
        q_lat = _dot(q[:, lo:hi].astype(BF16), ukt_ref[h])
        q_rope = q[:, hw + lo:hw + hi] * cos + q[:, 2 * hw + lo:2 * hw + hi] * sin
        q_ref[h, :, 0:W_CKV] = (q_lat * scale).astype(BF16)
        q_ref[h, :, W_CKV:QK_W] = (q_rope * scale).astype(BF16)
    c = _rms(ckv_ref[...], gkv_ref[...])
    kr = kr_ref[...] * cos + krr_ref[...] * sin
    ckvo_ref[...] = c
    kro_ref[...] = kr[:, :MLA_ROPE]
    kc_ref[:, 0:W_CKV] = c.astype(BF16)
    kc_ref[:, W_CKV:QK_W] = kr.astype(BF16)


def mla_prep(proj, cos, sin, g_q, g_kv, uq_all, ukt, *, tm=256):
    M = proj.shape[0]
    heads = ukt.shape[0]
    tm = _pick(M, tm)
    scale = float((MLA_NOPE + MLA_ROPE) ** -0.5 * math.log2(math.e))
    col = lambda off, w: (lambda i: (i, off // w))
    return pl.pallas_call(
        functools.partial(_mla_prep_kernel, heads=heads, scale=scale), name="mla_prep",
        grid=(M // tm,),
        in_specs=[
            pl.BlockSpec((tm, W_CQ), col(OFF_CQ, W_CQ)),
            pl.BlockSpec((tm, W_CKV), col(OFF_CKV, W_CKV)),
            pl.BlockSpec((tm, LANES), col(OFF_KR, LANES)),
            pl.BlockSpec((tm, LANES), col(OFF_KRR, LANES)),
            pl.BlockSpec((tm, LANES), lambda i: (i, 0)),
            pl.BlockSpec((tm, LANES), lambda i: (i, 0)),
            pl.BlockSpec((1, W_CQ), lambda i: (0, 0)),
            pl.BlockSpec((1, W_CKV), lambda i: (0, 0)),
            pl.BlockSpec(uq_all.shape, lambda i: (0, 0)),
            pl.BlockSpec(ukt.shape, lambda i: (0, 0, 0)),
        ],
        out_specs=[
            pl.BlockSpec((heads, tm, QK_W), lambda i: (0, i, 0)),
            pl.BlockSpec((tm, QK_W), lambda i: (i, 0)),
            pl.BlockSpec((tm, W_CKV), lambda i: (i, 0)),
            pl.BlockSpec((tm, MLA_ROPE), lambda i: (i, 0)),
        ],
        out_shape=[
            jax.ShapeDtypeStruct((heads, M, QK_W), BF16),
            jax.ShapeDtypeStruct((M, QK_W), BF16),
            jax.ShapeDtypeStruct((M, W_CKV), F32),
            jax.ShapeDtypeStruct((M, MLA_ROPE), F32),
        ],
        compiler_params=_cparams(("parallel",)),
    )(proj, proj, proj, proj, cos, sin, g_q.reshape(1, -1), g_kv.reshape(1, -1), uq_all, ukt)


def _softmax_step(s, v, m_prev, l_prev, acc_prev):
    m_new = jnp.maximum(m_prev, jnp.max(s, axis=-1, keepdims=True))
    alpha = jnp.exp2(m_prev - m_new)
    p = jnp.exp2(s - m_new)
    l_new = alpha * l_prev + jnp.sum(p, axis=-1, keepdims=True)
    acc_new = alpha * acc_prev + _dot(p.astype(BF16), v)
    return m_new, l_new, acc_new


def _mla_prompt_kernel(q_ref, k_ref, o_ref, m_sc, l_sc, acc_sc, *, heads, tq, tk):
    qi, ki = pl.program_id(1), pl.program_id(2)
    rows = heads * tq
    visible = ki * tk < (qi + 1) * tq
    on_diagonal = (ki + 1) * tk > qi * tq + 1

    @pl.when(ki == 0)
    def _():
        m_sc[...] = jnp.full_like(m_sc, NEG_BIG)
        l_sc[...] = jnp.zeros_like(l_sc)
        acc_sc[...] = jnp.zeros_like(acc_sc)

    def block(masked):
        k = k_ref[...]
        s = _dot_nt(q_ref[...].reshape(rows, QK_W), k)
        if masked:
            q_pos = qi * tq + lax.broadcasted_iota(jnp.int32, (tq, tk), 0)
            k_pos = ki * tk + lax.broadcasted_iota(jnp.int32, (tq, tk), 1)
            s = jnp.where((k_pos <= q_pos)[None], s.reshape(heads, tq, tk), NEG_BIG).reshape(rows, tk)
        m, l, acc = _softmax_step(s, k[:, 0:W_CKV], m_sc[...], l_sc[...], acc_sc[...])
        m_sc[...], l_sc[...], acc_sc[...] = m, l, acc

    pl.when(visible & on_diagonal)(lambda: block(True))
    pl.when(visible & jnp.logical_not(on_diagonal))(lambda: block(False))

    @pl.when(ki == pl.num_programs(2) - 1)
    def _():
        o = acc_sc[...] / l_sc[...]
        o_ref[...] = o.reshape(heads, tq, W_CKV).astype(o_ref.dtype)


def mla_attn_prompt(q, kc, batch, *, tq=256, tk=512):
    heads, M, _ = q.shape
    T = M // batch
    tq, tk = _pick(T, tq), _pick(T, tk)
    nq, nk = T // tq, T // tk
    kmap = lambda b, qi, ki: (b * nk + jnp.minimum(ki, ((qi + 1) * tq - 1) // tk), 0)
    return pl.pallas_call(
        functools.partial(_mla_prompt_kernel, heads=heads, tq=tq, tk=tk), name="mla_attn_prompt",
        grid=(batch, nq, nk),
        in_specs=[pl.BlockSpec((heads, tq, QK_W), lambda b, qi, ki: (0, b * nq + qi, 0)),
                  pl.BlockSpec((tk, QK_W), kmap)],
        out_specs=pl.BlockSpec((heads, tq, W_CKV), lambda b, qi, ki: (0, b * nq + qi, 0)),
        out_shape=jax.ShapeDtypeStruct((heads, M, W_CKV), BF16),
        scratch_shapes=[pltpu.VMEM((heads * tq, 1), F32), pltpu.VMEM((heads * tq, 1), F32),
                        pltpu.VMEM((heads * tq, W_CKV), F32)],
        compiler_params=_cparams(("parallel", "parallel", "arbitrary")),
    )(q, kc)


def _page_copies(pt_ref, hbm_refs, bufs, sems, layer, b, chunk, slot, pages_per_chunk):
    copies = []
    for i in range(pages_per_chunk):
        page = pt_ref[b, chunk * pages_per_chunk + i]
        for n, (hbm, buf) in enumerate(zip(hbm_refs, bufs)):
            copies.append(pltpu.make_async_copy(hbm.at[layer, page], buf.at[slot, i], sems.at[n, slot]))
    return copies


def _mla_sample_kernel(pt_ref, q_ref, knew_ref, lat_hbm, rope_hbm, o_ref, latbuf, ropebuf, sems, knew_pad,
                       *, layer, heads, t_new, pages_per_chunk, n_chunks, page):
    b = pl.program_id(0)
    nb = pl.num_programs(0)
    rows = heads * t_new
    hbm_refs, bufs = (lat_hbm, rope_hbm), (latbuf, ropebuf)
    copies = functools.partial(_page_copies, pt_ref, hbm_refs, bufs, sems, layer,
                               pages_per_chunk=pages_per_chunk)

    @pl.when(b == 0)
    def _():
        for c in copies(0, 0, 0):
            c.start()

    q = q_ref[0]
    q_lat, q_rope = q[:, 0:W_CKV], q[:, W_CKV:W_CKV + MLA_ROPE]
    keys = pages_per_chunk * page

    def chunk_step(c, carry):
        g = b * n_chunks + c
        slot = lax.rem(g, 2)
        last_chunk = c == n_chunks - 1
        nb_, nc_ = jnp.where(last_chunk, b + 1, b), jnp.where(last_chunk, 0, c + 1)

        @pl.when(g + 1 < nb * n_chunks)
        def _():
            for cp in copies(nb_, nc_, 1 - slot):
                cp.start()

        for cp in copies(b, c, slot):
            cp.wait()
        lat = latbuf[slot].reshape(keys, W_CKV).astype(BF16)
        s_rope = jnp.concatenate([_dot(q_rope, ropebuf[slot, i].astype(BF16)) for i in range(pages_per_chunk)],
                                 axis=1)
        s = _dot_nt(q_lat, lat) + s_rope
        return _softmax_step(s, lat, *carry)

    init = (jnp.full((rows, 1), NEG_BIG, F32), jnp.zeros((rows, 1), F32), jnp.zeros((rows, W_CKV), F32))
    m, l, acc = lax.fori_loop(0, n_chunks, chunk_step, init)

    knew_pad[...] = jnp.zeros_like(knew_pad)
    knew_pad[0:t_new, :] = knew_ref[0]
    kn = knew_pad[...]
    s = _dot_nt(q, kn)
    t_of_row = lax.broadcasted_iota(jnp.int32, s.shape, 0) % t_new
    j = lax.broadcasted_iota(jnp.int32, s.shape, 1)
    s = jnp.where((j <= t_of_row) & (j < t_new), s, NEG_BIG)
    m, l, acc = _softmax_step(s, kn[:, 0:W_CKV], m, l, acc)
    o_ref[0] = (acc / l).astype(o_ref.dtype)


def mla_attn_sample(page_table, q, knew, cache_lat, cache_rope_t, layer, *, pages_per_chunk=32):
    bs, rows, _ = q.shape
    t_new = knew.shape[1]
    heads = rows // t_new
    n_pages = page_table.shape[1]
    page = cache_lat.shape[2]
    ppc = _pick(n_pages, pages_per_chunk)
    kern = functools.partial(_mla_sample_kernel, layer=layer, heads=heads, t_new=t_new,
                             pages_per_chunk=ppc, n_chunks=n_pages // ppc, page=page)
    grid_spec = pltpu.PrefetchScalarGridSpec(
        num_scalar_prefetch=1,
        grid=(bs,),
        in_specs=[
            pl.BlockSpec((1, rows, QK_W), lambda b, pt: (b, 0, 0)),
            pl.BlockSpec((1, t_new, QK_W), lambda b, pt: (b, 0, 0)),
            pl.BlockSpec(memory_space=pl.ANY),
            pl.BlockSpec(memory_space=pl.ANY),
        ],
        out_specs=pl.BlockSpec((1, rows, W_CKV), lambda b, pt: (b, 0, 0)),
        scratch_shapes=[
            pltpu.VMEM((2, ppc, page, W_CKV), F32),
            pltpu.VMEM((2, ppc, MLA_ROPE, page), F32),
            pltpu.SemaphoreType.DMA((2, 2)),
            pltpu.VMEM((LANES, QK_W), BF16),
        ],
    )
    return pl.pallas_call(
        kern, grid_spec=grid_spec, name="mla_attn_sample",
        out_shape=jax.ShapeDtypeStruct((bs, rows, W_CKV), BF16),
        compiler_params=_cparams(("arbitrary",)),
    )(page_table, q, knew, cache_lat, cache_rope_t)


def _mla_post_kernel(o_ref, uv_ref, g_ref, out_ref, *, heads):
    for h in range(heads):
        y = _dot(o_ref[h], uv_ref[h])
        out_ref[:, h * MLA_V:(h + 1) * MLA_V] = _rms(y, g_ref[h:h + 1, :]).astype(out_ref.dtype)


def mla_post(o_lat, uv, g_out, *, tm=512):
    heads, M, _ = o_lat.shape
    tm = _pick(M, tm)
    return pl.pallas_call(
        functools.partial(_mla_post_kernel, heads=heads), name="mla_post",
        grid=(M // tm,),
        in_specs=[pl.BlockSpec((heads, tm, W_CKV), lambda i: (0, i, 0)),
                  pl.BlockSpec(uv.shape, lambda i: (0, 0, 0)),
                  pl.BlockSpec(g_out.shape, lambda i: (0, 0))],
        out_specs=pl.BlockSpec((tm, heads * MLA_V), lambda i: (i, 0)),
        out_shape=jax.ShapeDtypeStruct((M, heads * MLA_V), BF16),
        compiler_params=_cparams(("parallel",)),
    )(o_lat, uv, g_out)


def _suffix_matrix(n):
    r = lax.broadcasted_iota(jnp.int32, (n, n), 0)
    c = lax.broadcasted_iota(jnp.int32, (n, n), 1)
    return jnp.where(r > c, 1.0, 0.0).astype(BF16)


def _suffix_sums(lk, u):
    hi = lk.astype(BF16)
    mid = (lk - hi.astype(F32)).astype(BF16)
    return _dot(hi, u) + _dot(mid, u)


def _sb_block(q, k, v, mask, u, carry, acc):
    z = _dot_nt(q, k)
    lp = _log_sigmoid(z)
    lk = lp - z
    if mask is not None:
        lk = jnp.where(mask, lk, 0.0)
    la = _suffix_sums(lk, u)
    a = jnp.exp(lp + la + carry)
    if mask is not None:
        a = jnp.where(mask, a, 0.0)
    acc = acc + _dot(a.astype(BF16), v)
    carry = carry + la[:, 0:1] + lk[:, 0:1]
    return carry, acc


def _sb_prompt_kernel(q_ref, k_ref, v_ref, g_ref, o_ref, qs_sc, carry_sc, acc_sc, *, heads, tq, tk, scale):
    qi, ki = pl.program_id(1), pl.program_id(2)
    last = ((qi + 1) * tq - 1) // tk
    rows = heads * tq

    @pl.when(ki == 0)
    def _():
        q = q_ref[...]
        for h in range(heads):
            qs_sc[h * tq:(h + 1) * tq, :] = (q[:, h * SB_D:(h + 1) * SB_D] * scale).astype(BF16)
        carry_sc[...] = jnp.zeros_like(carry_sc)
        acc_sc[...] = jnp.zeros_like(acc_sc)

    @pl.when(ki <= last)
    def _():
        kb = last - ki
        q_pos = qi * tq + (lax.broadcasted_iota(jnp.int32, (rows, tk), 0) & (tq - 1))
        k_pos = kb * tk + lax.broadcasted_iota(jnp.int32, (rows, tk), 1)
        carry, acc = _sb_block(qs_sc[...], k_ref[...].astype(BF16), v_ref[...].astype(BF16), k_pos < q_pos,
                               _suffix_matrix(tk), carry_sc[...], acc_sc[...])
        carry_sc[...], acc_sc[...] = carry, acc

    @pl.when(ki == pl.num_programs(2) - 1)
    def _():
        for h in range(heads):
            o = _rms(acc_sc[h * tq:(h + 1) * tq, :], g_ref[h:h + 1, :])
            o_ref[:, h * SB_D:(h + 1) * SB_D] = o.astype(o_ref.dtype)


def sb_attn_prompt(proj, g_out, batch, *, tq=128, tk=256):
    M = proj.shape[0]
    heads = g_out.shape[0]
    T = M // batch
    tq, tk = _pick(T, tq), _pick(T, tk)
    assert tq & (tq - 1) == 0
    nq, nk = T // tq, T // tk

    def kvmap(colblock):
        def f(b, qi, ki):
            last = ((qi + 1) * tq - 1) // tk
            return (b * nk + jnp.maximum(last - ki, 0), colblock)
        return f

    return pl.pallas_call(
        functools.partial(_sb_prompt_kernel, heads=heads, tq=tq, tk=tk, scale=float(SB_D ** -0.5)),
        name="sb_attn_prompt",
        grid=(batch, nq, nk),
        in_specs=[pl.BlockSpec((tq, W_SQ), lambda b, qi, ki: (b * nq + qi, OFF_SQ // W_SQ)),
                  pl.BlockSpec((tk, SB_D), kvmap(OFF_SK // SB_D)),
                  pl.BlockSpec((tk, SB_D), kvmap(OFF_SV // SB_D)),
                  pl.BlockSpec(g_out.shape, lambda b, qi, ki: (0, 0))],
        out_specs=pl.BlockSpec((tq, heads * SB_D), lambda b, qi, ki: (b * nq + qi, 0)),
        out_shape=jax.ShapeDtypeStruct((M, heads * SB_D), BF16),
        scratch_shapes=[pltpu.VMEM((heads * tq, SB_D), BF16), pltpu.VMEM((heads * tq, 1), F32),
                        pltpu.VMEM((heads * tq, SB_D), F32)],
        compiler_params=_cparams(("parallel", "parallel", "arbitrary")),
    )(proj, proj, proj, g_out)


def _sb_sample_kernel(pt_ref, q_ref, knew_ref, vnew_ref, g_ref, k_hbm, v_hbm, o_ref, kbuf, vbuf, sems,
                      knew_pad, vnew_pad, *, layer, t_new, pages_per_chunk, n_chunks, page, sub, scale):
    b = pl.program_id(0)
    nb = pl.num_programs(0)
    rows = q_ref.shape[1]
    copies = functools.partial(_page_copies, pt_ref, (k_hbm, v_hbm), (kbuf, vbuf), sems, layer,
                               pages_per_chunk=pages_per_chunk)
    chunk_of = lambda c: n_chunks - 1 - c

    @pl.when(b == 0)
    def _():
        for cp in copies(0, chunk_of(0), 0):
            cp.start()

    q = (q_ref[0] * scale).astype(BF16)

    knew_pad[...] = jnp.zeros_like(knew_pad)
    vnew_pad[...] = jnp.zeros_like(vnew_pad)
    knew_pad[0:t_new, :] = knew_ref[0]
    vnew_pad[0:t_new, :] = vnew_ref[0]
    t_of_row = lax.broadcasted_iota(jnp.int32, (rows, LANES), 0) % t_new
    j = lax.broadcasted_iota(jnp.int32, (rows, LANES), 1)
    carry, acc = _sb_block(q, knew_pad[...].astype(BF16), vnew_pad[...].astype(BF16), j < t_of_row,
                           _suffix_matrix(LANES), jnp.zeros((rows, 1), F32), jnp.zeros((rows, SB_D), F32))

    keys = sub * page
    u = _suffix_matrix(keys)

    def chunk_step(c, state):
        g = b * n_chunks + c
        slot = lax.rem(g, 2)
        last_chunk = c == n_chunks - 1
        nb_, nc_ = jnp.where(last_chunk, b + 1, b), jnp.where(last_chunk, 0, c + 1)

        @pl.when(g + 1 < nb * n_chunks)
        def _():
            for cp in copies(nb_, chunk_of(nc_), 1 - slot):
                cp.start()

        for cp in copies(b, chunk_of(c), slot):
            cp.wait()
        carry, acc = state
        nblk = pages_per_chunk // sub
        k = kbuf[slot].reshape(nblk * keys, SB_D).astype(BF16)
        v = vbuf[slot].reshape(nblk * keys, SB_D).astype(BF16)
        z_wide = _dot_nt(q, k)
        z = jnp.concatenate([z_wide[:, i * keys:(i + 1) * keys] for i in range(nblk)], axis=0)
        lp = _log_sigmoid(z)
        lk = lp - z
        la = _suffix_sums(lk, u)
        tot = la[:, 0:1] + lk[:, 0:1]
        after = [None] * nblk
        for i in reversed(range(nblk)):
            after[i] = carry
            carry = carry + tot[i * rows:(i + 1) * rows]
        a = jnp.exp(lp + la + jnp.concatenate(after, axis=0))
        a_wide = jnp.concatenate([a[i * rows:(i + 1) * rows] for i in range(nblk)], axis=1)
        acc = acc + _dot(a_wide.astype(BF16), v)
        return carry, acc

    carry, acc = lax.fori_loop(0, n_chunks, chunk_step, (carry, acc))
    o_ref[0] = _rms(acc, g_ref[...]).astype(o_ref.dtype)


def sb_attn_sample(page_table, q, knew, vnew, g_rows, cache_k, cache_v, layer, *, pages_per_chunk=32, sub=2):
    bs, rows, _ = q.shape
    t_new = knew.shape[1]
    n_pages = page_table.shape[1]
    page = cache_k.shape[2]
    ppc = _pick(n_pages, pages_per_chunk)
    sub = _pick(ppc, sub)
    kern = functools.partial(_sb_sample_kernel, layer=layer, t_new=t_new, pages_per_chunk=ppc,
                             n_chunks=n_pages // ppc, page=page, sub=sub, scale=float(SB_D ** -0.5))
    grid_spec = pltpu.PrefetchScalarGridSpec(
        num_scalar_prefetch=1,
        grid=(bs,),
        in_specs=[
            pl.BlockSpec((1, rows, SB_D), lambda b, pt: (b, 0, 0)),
            pl.BlockSpec((1, t_new, SB_D), lambda b, pt: (b, 0, 0)),
            pl.BlockSpec((1, t_new, SB_D), lambda b, pt: (b, 0, 0)),
            pl.BlockSpec((rows, SB_D), lambda b, pt: (0, 0)),
            pl.BlockSpec(memory_space=pl.ANY),
            pl.BlockSpec(memory_space=pl.ANY),
        ],
        out_specs=pl.BlockSpec((1, rows, SB_D), lambda b, pt: (b, 0, 0)),
        scratch_shapes=[
            pltpu.VMEM((2, ppc, page, SB_D), F32),
            pltpu.VMEM((2, ppc, page, SB_D), F32),
            pltpu.SemaphoreType.DMA((2, 2)),
            pltpu.VMEM((LANES, SB_D), F32),
            pltpu.VMEM((LANES, SB_D), F32),
        ],
    )
    return pl.pallas_call(
        kern, grid_spec=grid_spec, name="sb_attn_sample",
        out_shape=jax.ShapeDtypeStruct((bs, rows, SB_D), BF16),
        compiler_params=_cparams(("arbitrary",)),
    )(page_table, q, knew, vnew, g_rows, cache_k, cache_v)


def _split(x):
    hi = x.astype(BF16)
    return hi, (x - hi.astype(F32)).astype(BF16)


def _split3(x):
    hi = x.astype(BF16)
    rest = x - hi.astype(F32)
    mid = rest.astype(BF16)
    return hi, mid, (rest - mid.astype(F32)).astype(BF16)


def _dot3(a, b, dims=NN_DIMS):
    ah, al = a if isinstance(a, tuple) else _split(a)
    bh, bl = b if isinstance(b, tuple) else _split(b)
    d = lambda x, y: lax.dot_general(x, y, dims, preferred_element_type=F32)
    return d(ah, bh) + d(ah, bl) + d(al, bh)


def _dn_kernel(qkv_ref, z_ref, ab_ref, prev_ref, s0_ref, wconv_ref, alog_ref, dtb_ref, gout_ref,
               o_ref, snew_ref, xe_sc, s_sc, abp_sc, zp_sc, *, heads, chunk, t_blk, rows_per_step):
    n = pl.program_id(1)
    C = chunk
    hk = heads * DN_DK
    padded = t_blk < C

    @pl.when(n == 0)
    def _():
        for g in range(rows_per_step):
            xe_sc[g, 0:SUBLANES, :] = jnp.zeros((SUBLANES, xe_sc.shape[2]), F32)
            xe_sc[g, SUBLANES - (DN_CONV - 1):SUBLANES, :] = prev_ref[g]
            s_sc[g * heads:(g + 1) * heads] = s0_ref[g]

    row = lax.broadcasted_iota(jnp.int32, (C, LANES), 0)
    lane = lax.broadcasted_iota(jnp.int32, (C, LANES), 1)
    valid = row < t_blk
    r = lax.broadcasted_iota(jnp.int32, (C, C), 0)
    c = lax.broadcasted_iota(jnp.int32, (C, C), 1)
    incl, strict = r >= c, r > c
    lower = jnp.where(incl, 1.0, 0.0).astype(BF16)
    eye = jnp.where(r == c, 1.0, 0.0)
    sel = jnp.where(lax.broadcasted_iota(jnp.int32, (SUBLANES, LANES), 0)
                    == lax.broadcasted_iota(jnp.int32, (SUBLANES, LANES), 1), 1.0, 0.0).astype(BF16)
    w = wconv_ref[...]
    rows = [_dn_row_inputs(g, qkv_ref, z_ref, ab_ref, alog_ref, dtb_ref, xe_sc, abp_sc, zp_sc, w, valid, lane,
                           lower, sel, heads=heads, C=C, t_blk=t_blk, padded=padded)
            for g in range(rows_per_step)]
    _dn_chains(rows, gout_ref, o_ref, s_sc, valid, incl, strict, eye, heads=heads, C=C, t_blk=t_blk,
               padded=padded)

    @pl.when(n == pl.num_programs(1) - 1)
    def _():
        for g in range(rows_per_step):
            snew_ref[g] = s_sc[g * heads:(g + 1) * heads]


def _dn_row_inputs(g, qkv_ref, z_ref, ab_ref, alog_ref, dtb_ref, xe_sc, abp_sc, zp_sc, w, valid, lane, lower, sel,
                   *, heads, C, t_blk, padded):
    if padded:
        xe_sc[g, SUBLANES:SUBLANES + C, :] = jnp.zeros((C, xe_sc.shape[2]), F32)
        abp_sc[g] = jnp.zeros(abp_sc.shape[1:], F32)
        zp_sc[g] = jnp.zeros(zp_sc.shape[1:], F32)
        abp_sc[g, 0:t_blk, :] = ab_ref[g]
        zp_sc[g, 0:t_blk, :] = z_ref[g]
        ab, zg = abp_sc[g], zp_sc[g]
    else:
        ab, zg = ab_ref[g], z_ref[g]
    xe_sc[g, SUBLANES:SUBLANES + t_blk, :] = qkv_ref[g]

    conv = w[DN_CONV - 1:DN_CONV, :] * xe_sc[g, SUBLANES:SUBLANES + C, :]
    for i in range(1, DN_CONV):
        conv = conv + w[DN_CONV - 1 - i:DN_CONV - i, :] * xe_sc[g, SUBLANES - i:SUBLANES - i + C, :]
    if not padded:
        xe_sc[g, 0:SUBLANES, :] = xe_sc[g, C:C + SUBLANES, :]
    act = conv * jax.nn.sigmoid(conv)

    g4 = -jnp.exp(alog_ref[...]) * _softplus(ab + dtb_ref[...])
    g4 = jnp.where(valid & (lane < heads), g4, 0.0)
    beta4 = jnp.where(valid, jax.nn.sigmoid(ab), 0.0)

    g_cum = sum(_dot(lower, part) for part in _split3(g4))
    g_cum_t = sum(_dot_nt(sel, part) for part in _split3(g_cum))
    return act, zg, beta4, g_cum, g_cum_t


def _dn_chains(rows, gout_ref, o_ref, s_sc, valid, incl, strict, eye, *, heads, C, t_blk, padded):
    hk = heads * DN_DK
    chains = [(g, h) for g in range(len(rows)) for h in range(heads)]
    q, k, v, gc, beta, decay = {}, {}, {}, {}, {}, {}
    for ch in chains:
        g, h = ch
        act, _, beta4, g_cum, g_cum_t = rows[g]
        qh = act[:, h * DN_DK:(h + 1) * DN_DK]
        kh = act[:, hk + h * DN_DK:hk + (h + 1) * DN_DK]
        vh = act[:, 2 * hk + h * DN_DV:2 * hk + (h + 1) * DN_DV]
        qh = qh * lax.rsqrt(jnp.sum(qh * qh, axis=-1, keepdims=True) + L2_EPS) * (DN_DK ** -0.5)
        kh = kh * lax.rsqrt(jnp.sum(kh * kh, axis=-1, keepdims=True) + L2_EPS)
        if padded:
            qh = jnp.where(valid, qh, 0.0)
            kh = jnp.where(valid, kh, 0.0)
            vh = jnp.where(valid, vh, 0.0)
        q[ch], k[ch], v[ch] = qh, kh, vh
        gc[ch] = g_cum[:, h:h + 1]
        beta[ch] = beta4[:, heads + h:heads + h + 1]
        decay[ch] = jnp.where(incl, jnp.exp(jnp.minimum(gc[ch] - g_cum_t[h:h + 1, :], 0.0)), 0.0)

    qkk = {ch: _dot3(jnp.concatenate([q[ch], k[ch]], axis=0), k[ch], NT_DIMS) for ch in chains}
    qk = {ch: qkk[ch][:C] * decay[ch] for ch in chains}
    a = {ch: jnp.where(strict, beta[ch] * qkk[ch][C:] * decay[ch], 0.0) for ch in chains}
    tinv = {ch: eye - a[ch] for ch in chains}
    p = {ch: _dot3(a[ch], a[ch]) for ch in chains}
    n_factors = int(math.log2(C)) - 1
    for i in range(n_factors):
        if i < n_factors - 1:
            both = {ch: _dot3(jnp.concatenate([p[ch], tinv[ch]], axis=0), _split(p[ch])) for ch in chains}
            p = {ch: both[ch][:C] for ch in chains}
            tinv = {ch: tinv[ch] + both[ch][C:] for ch in chains}
        else:
            tinv = {ch: tinv[ch] + _dot3(tinv[ch], p[ch]) for ch in chains}
    e_g = {ch: jnp.exp(gc[ch]) for ch in chains}
    uw = {ch: _dot3(tinv[ch], jnp.concatenate([beta[ch] * v[ch], (beta[ch] * e_g[ch]) * k[ch]], axis=1))
          for ch in chains}
    g_last = {ch: gc[ch][C - 1:C, :] for ch in chains}
    s = {(g, h): s_sc[g * heads + h] for g, h in chains}
    ws_qs = {ch: _dot3(jnp.concatenate([uw[ch][:, DN_DV:], q[ch] * e_g[ch]], axis=0), s[ch]) for ch in chains}
    delta = {ch: uw[ch][:, :DN_DV] - ws_qs[ch][:C] for ch in chains}
    o = {ch: ws_qs[ch][C:] + _dot3(qk[ch], delta[ch]) for ch in chains}
    s_new = {ch: jnp.exp(g_last[ch]) * s[ch]
             + _dot3(k[ch] * jnp.exp(g_last[ch] - gc[ch]), delta[ch], TN_DIMS) for ch in chains}
    for g, h in chains:
        s_sc[g * heads + h] = s_new[(g, h)]
        zh = rows[g][1][:, h * DN_DV:(h + 1) * DN_DV]
        out = _rms(o[(g, h)], gout_ref[...]) * (zh * jax.nn.sigmoid(zh))
        o_ref[g, :, h * DN_DV:(h + 1) * DN_DV] = out[0:t_blk].astype(o_ref.dtype)


def deltanet(proj4, prev_conv, s0, w_conv, a_log, dt_bias, g_out, *, chunk, rows_per_step):
    batch, n_chunks, t_blk, _ = proj4.shape
    heads = s0.shape[1]
    assert w_conv.shape[1] == W_QKV and heads * DN_DV == W_Z and 2 * heads <= SUBLANES
    G = _pick(batch, rows_per_step)
    pad = lambda v: jnp.zeros((1, LANES), F32).at[0, :heads].set(v)
    gmap = lambda colblock: (lambda b, n: (b, n, 0, colblock))
    return pl.pallas_call(
        functools.partial(_dn_kernel, heads=heads, chunk=chunk, t_blk=t_blk, rows_per_step=G), name="deltanet",
        grid=(batch // G, n_chunks),
        in_specs=[
            pl.BlockSpec((G, None, t_blk, W_QKV), gmap(OFF_QKV // W_QKV)),
            pl.BlockSpec((G, None, t_blk, W_Z), gmap(OFF_Z // W_Z)),
            pl.BlockSpec((G, None, t_blk, LANES), gmap(OFF_AB // LANES)),
            pl.BlockSpec((G, DN_CONV - 1, W_QKV), lambda b, n: (b, 0, 0)),
            pl.BlockSpec((G, heads, DN_DK, DN_DV), lambda b, n: (b, 0, 0, 0)),
            pl.BlockSpec((DN_CONV, W_QKV), lambda b, n: (0, 0)),
            pl.BlockSpec((1, LANES), lambda b, n: (0, 0)),
            pl.BlockSpec((1, LANES), lambda b, n: (0, 0)),
            pl.BlockSpec((1, DN_DV), lambda b, n: (0, 0)),
        ],
        out_specs=[
            pl.BlockSpec((G, None, t_blk, W_Z), lambda b, n: (b, n, 0, 0)),
            pl.BlockSpec((G, heads, DN_DK, DN_DV), lambda b, n: (b, 0, 0, 0)),
        ],
        out_shape=[
            jax.ShapeDtypeStruct((batch, n_chunks, t_blk, W_Z), BF16),
            jax.ShapeDtypeStruct((batch, heads, DN_DK, DN_DV), F32),
        ],
        scratch_shapes=[
            pltpu.VMEM((G, chunk + SUBLANES, W_QKV), F32),
            pltpu.VMEM((G * heads, DN_DK, DN_DV), F32),
            pltpu.VMEM((G, chunk, LANES), F32),
            pltpu.VMEM((G, chunk, W_Z), F32),
        ],
        compiler_params=_cparams(("parallel", "arbitrary")),
    )(proj4, proj4, proj4, prev_conv, s0, w_conv, pad(a_log), pad(dt_bias), g_out.reshape(1, -1))


def _pack_w_in(w, dims):
    ql, kvl, dnh, sbh = dims
    sizes = (ql, kvl, MLA_ROPE, 3 * dnh * DN_DK, dnh * DN_DV, dnh, dnh, sbh * SB_D, SB_D, SB_D)
    parts, off = [], 0
    for n in sizes:
        parts.append(w[:, off:off + n])
        off += n
    cq, ckv, kr, qkv, z, a, b, sq, sk, sv = parts
    half = MLA_ROPE // 2
    kr_rot = jnp.concatenate([-kr[:, half:], kr[:, :half]], axis=1)
    zeros = lambda n: jnp.zeros((w.shape[0], n), w.dtype)
    cols = [qkv, cq, sk, z, sq, ckv, sv, kr, zeros(LANES - MLA_ROPE), kr_rot, zeros(LANES - MLA_ROPE),
            a, b, zeros(LANES - 2 * dnh)]
    packed = jnp.concatenate(cols, axis=1)
    assert packed.shape[1] == NP_IN, packed.shape
    return packed.astype(BF16)


def _pack_uq(uq):
    ql, heads, _ = uq.shape
    half = MLA_ROPE // 2
    nope = uq[:, :, :MLA_NOPE].reshape(ql, heads * MLA_NOPE)
    rope = uq[:, :, MLA_NOPE:]
    rot = jnp.concatenate([-rope[:, :, half:], rope[:, :, :half]], axis=2)
    padz = jnp.zeros((ql, heads, LANES - MLA_ROPE), uq.dtype)
    rope_p = jnp.concatenate([rope, padz], axis=2).reshape(ql, heads * LANES)
    rot_p = jnp.concatenate([rot, padz], axis=2).reshape(ql, heads * LANES)
    return jnp.concatenate([nope, rope_p, rot_p], axis=1).astype(BF16)


def _rope_tables(pos):
    half = MLA_ROPE // 2
    inv_freq = ROPE_THETA ** (-jnp.arange(half, dtype=F32) / half)
    ang = pos.astype(F32)[:, None] * inv_freq[None, :]
    z = jnp.zeros((pos.shape[0], LANES - MLA_ROPE), F32)
    cos = jnp.concatenate([jnp.cos(ang), jnp.cos(ang), z], axis=1)
    sin = jnp.concatenate([jnp.sin(ang), jnp.sin(ang), z], axis=1)
    return cos, sin


def _layer(x, h, lw, group):
    M = x.shape[0]
    batch, t_len = group["batch"], group["t"]
    proj = matmul(h, lw["w_in"], F32, tm=1024, tn=1280)
    q, kc, ckv_o, kr_o = mla_prep(proj, group["cos"], group["sin"], lw["g_mla_q"], lw["g_mla_kv"],
                                  lw["uq"], lw["ukt"])
    heads = q.shape[0]
    sbh = lw["g_sb_out"].shape[0]
    if group["paged"] is None:
        o_lat = mla_attn_prompt(q, kc, batch)
        o_sb = sb_attn_prompt(proj, lw["g_sb_out"], batch)
        n_chunks, chunk, dn_rows = t_len // DN_CHUNK, DN_CHUNK, 2
        prev_conv = jnp.zeros((batch, DN_CONV - 1, W_QKV), F32)
        s0 = jnp.zeros((batch, W_Z // DN_DV, DN_DK, DN_DV), F32)
    else:
        pg = group["paged"]
        layer = pg["layer"]
        to_rows = lambda a, nh: a.reshape(nh, batch, t_len, a.shape[-1]).transpose(1, 0, 2, 3).reshape(
            batch, nh * t_len, a.shape[-1])
        from_rows = lambda a, nh: a.reshape(batch, nh, t_len, a.shape[-1]).transpose(1, 0, 2, 3).reshape(
            nh, M, a.shape[-1])
        o_rows = mla_attn_sample(pg["page_table"], to_rows(q, heads), kc.reshape(batch, t_len, QK_W),
                                 pg["lat"], pg["rope"], layer)
        o_lat = from_rows(o_rows, heads)
        sq = proj[:, OFF_SQ:OFF_SQ + W_SQ].reshape(batch, t_len, sbh, SB_D).transpose(0, 2, 1, 3)
        sk = proj[:, OFF_SK:OFF_SK + SB_D].reshape(batch, t_len, SB_D)
        sv = proj[:, OFF_SV:OFF_SV + SB_D].reshape(batch, t_len, SB_D)
        g_rows = jnp.repeat(lw["g_sb_out"], t_len, axis=0)
        o_sb_rows = sb_attn_sample(pg["page_table"], sq.reshape(batch, sbh * t_len, SB_D), sk, sv, g_rows,
                                   pg["sbk"], pg["sbv"], layer)
        o_sb = o_sb_rows.reshape(batch, sbh, t_len, SB_D).transpose(0, 2, 1, 3).reshape(M, sbh * SB_D)
        n_chunks, chunk, dn_rows = 1, SUBLANES, 4
        assert t_len <= chunk
        prev_conv, s0 = pg["conv"], pg["S"]
    o_mla = mla_post(o_lat, lw["uv"], lw["g_mla_out"])
    o_dn, s_new = deltanet(proj.reshape(batch, n_chunks, t_len // n_chunks, NP_IN), prev_conv, s0, lw["w_conv"],
                           lw["a_log"], lw["dt_bias"], lw["g_dn_out"], chunk=chunk, rows_per_step=dn_rows)
    mixed = jnp.concatenate([o_mla, o_dn.reshape(M, W_Z), o_sb], axis=1)
    x1, hm = matmul_res_norm(mixed, lw["w_out"], x, lw["g_post_mix"], lw["g_pre_mlp"])
    ff = matmul(hm, lw["w_up"], BF16, relu2=True, tm=1024, tn=1024)
    x2, h_next = matmul_res_norm(ff, lw["w_down"], x1, lw["g_post_mlp"], lw["g_next"])
    proj_b = proj.reshape(batch, t_len, NP_IN)
    state = (ckv_o.reshape(batch, t_len, W_CKV), kr_o.reshape(batch, t_len, MLA_ROPE),
             proj_b[:, :, OFF_SK:OFF_SK + SB_D], proj_b[:, :, OFF_SV:OFF_SV + SB_D], s_new,
             proj_b[:, t_len - (DN_CONV - 1):, OFF_QKV:OFF_QKV + W_QKV])
    return x2, h_next, state


def kernel(x_prompt, x_sample, cache_mla_latent, cache_mla_rope, cache_sb_k, cache_sb_v, state_dn_S, state_dn_conv, page_table, w_in, g_pre_mix, g_mla_q, w_mla_uq, g_mla_kv, w_mla_uk, w_mla_uv, g_mla_out, w_dn_conv, dn_A_log, dn_dt_bias, g_dn_out, g_sb_out, w_out, g_post_mix, g_pre_mlp, w_up, w_down, g_post_mlp):
    depth = w_in.shape[0]
    B, T, D = x_prompt.shape
    Bs, Ts, _ = x_sample.shape
    past_len = page_table.shape[1] * cache_mla_latent.shape[2]
    dims = (w_mla_uq.shape[1], w_mla_uk.shape[1], dn_A_log.shape[1], g_sb_out.shape[1])
    assert dims == (W_CQ, W_CKV, W_Z // DN_DV, W_SQ // SB_D) and T % DN_CHUNK == 0

    rope_t = jnp.swapaxes(cache_mla_rope, 2, 3)
    cos_p, sin_p = _rope_tables(jnp.tile(jnp.arange(T), B))
    cos_s, sin_s = _rope_tables(jnp.tile(past_len + jnp.arange(Ts), Bs))
    groups = [dict(batch=B, t=T, cos=cos_p, sin=sin_p, paged=None),
              dict(batch=Bs, t=Ts, cos=cos_s, sin=sin_s, paged=None)]
    xs = [x_prompt.reshape(B * T, D), x_sample.reshape(Bs * Ts, D)]
    hs = [norm_cast(x, g_pre_mix[0]) for x in xs]
    states = [[], []]
    for l in range(depth):
        lw = dict(
            w_in=_pack_w_in(w_in[l], dims), g_mla_q=g_mla_q[l], g_mla_kv=g_mla_kv[l],
            uq=_pack_uq(w_mla_uq[l]), ukt=w_mla_uk[l].transpose(1, 2, 0).astype(BF16),
            uv=w_mla_uv[l].transpose(1, 0, 2).astype(BF16), g_mla_out=g_mla_out[l],
            w_conv=w_dn_conv[l], a_log=dn_A_log[l], dt_bias=dn_dt_bias[l], g_dn_out=g_dn_out[l],
            g_sb_out=g_sb_out[l], w_out=w_out[l].astype(BF16), g_post_mix=g_post_mix[l],
            g_pre_mlp=g_pre_mlp[l], w_up=w_up[l].astype(BF16), w_down=w_down[l].astype(BF16),
            g_post_mlp=g_post_mlp[l], g_next=g_pre_mix[(l + 1) % depth])
        groups[1]["paged"] = dict(layer=l, page_table=page_table, lat=cache_mla_latent, rope=rope_t,
                                  sbk=cache_sb_k, sbv=cache_sb_v, S=state_dn_S[l], conv=state_dn_conv[l])
        for gi in range(2):
            xs[gi], hs[gi], st = _layer(xs[gi], hs[gi], lw, groups[gi])
            states[gi].append(st)
    p_state = tuple(jnp.stack(t) for t in zip(*states[0]))
    s_state = tuple(jnp.stack(t) for t in zip(*states[1]))
    return (xs[0].reshape(B, T, D), xs[1].reshape(Bs, Ts, D)) + p_state + s_state
```

```python
import functools
import math

import jax
import jax.numpy as jnp
from jax import lax
from jax.experimental import pallas as pl
from jax.experimental.pallas import tpu as pltpu

F32 = jnp.float32
BF16 = jnp.bfloat16

MLA_NOPE = 128
MLA_ROPE = 64
MLA_V = 128
DN_DK = 128
DN_DV = 128
DN_CONV = 4
DN_CHUNK = 64
SB_D = 128
NORM_EPS = 1e-6
L2_EPS = 1e-6
NEG_BIG = -1e30
ROPE_THETA = 10000.0

LANES = 128
SUBLANES = 8
VMEM_LIMIT = 48 * 1024 * 1024

OFF_QKV, W_QKV = 0, 1536
OFF_CQ, W_CQ = 1536, 384
OFF_SK = 1920
OFF_Z, W_Z = 2048, 512
OFF_SQ, W_SQ = 2560, 512
OFF_CKV, W_CKV = 3072, 256
OFF_SV = 3328
OFF_KR = 3456
OFF_KRR = 3584
OFF_AB = 3712
NP_IN = 3840
QK_W = 384

NN_DIMS = (((1,), (0,)), ((), ()))
NT_DIMS = (((1,), (1,)), ((), ()))
TN_DIMS = (((0,), (0,)), ((), ()))


def _cparams(sem):
    return pltpu.CompilerParams(dimension_semantics=sem, vmem_limit_bytes=VMEM_LIMIT)


def _rms(x, g):
    return x * lax.rsqrt(jnp.mean(x * x, axis=-1, keepdims=True) + NORM_EPS) * g


def _dot(a, b):
    return jnp.dot(a, b, preferred_element_type=F32)


def _dot_nt(a, b):
    return lax.dot_general(a, b, NT_DIMS, preferred_element_type=F32)


def _log_sigmoid(z):
    return jnp.minimum(z, 0.0) - jnp.log1p(jnp.exp(-jnp.abs(z)))


def _softplus(x):
    return jnp.maximum(x, 0.0) + jnp.log1p(jnp.exp(-jnp.abs(x)))


def _pick(total, want):
    t = min(total, want)
    assert total % t == 0, (total, want)
    return t


def _norm_cast_kernel(x_ref, g_ref, o_ref):
    o_ref[...] = _rms(x_ref[...], g_ref[...]).astype(o_ref.dtype)


def norm_cast(x, g):
    M, D = x.shape
    tm = _pick(M, 512)
    return pl.pallas_call(
        _norm_cast_kernel, name="norm_cast",
        grid=(M // tm,),
        in_specs=[pl.BlockSpec((tm, D), lambda i: (i, 0)), pl.BlockSpec((1, D), lambda i: (0, 0))],
        out_specs=pl.BlockSpec((tm, D), lambda i: (i, 0)),
        out_shape=jax.ShapeDtypeStruct((M, D), BF16),
        compiler_params=_cparams(("parallel",)),
    )(x, g.reshape(1, D))


def _mm_kernel(a_ref, w_ref, o_ref, *, relu2):
    acc = _dot(a_ref[...], w_ref[...])
    if relu2:
        acc = jnp.square(jnp.maximum(acc, 0.0))
    o_ref[...] = acc.astype(o_ref.dtype)


def matmul(a, w, out_dtype, *, relu2=False, tm=1024, tn=1024):
    M, K = a.shape
    N = w.shape[1]
    tm, tn = _pick(M, tm), _pick(N, tn)
    return pl.pallas_call(
        functools.partial(_mm_kernel, relu2=relu2), name="matmul_relu2" if relu2 else "matmul",
        grid=(M // tm, N // tn),
        in_specs=[pl.BlockSpec((tm, K), lambda i, j: (i, 0)), pl.BlockSpec((K, tn), lambda i, j: (0, j))],
        out_specs=pl.BlockSpec((tm, tn), lambda i, j: (i, j)),
        out_shape=jax.ShapeDtypeStruct((M, N), out_dtype),
        compiler_params=_cparams(("parallel", "arbitrary")),
    )(a, w)


def _mm_res_norm_kernel(a_ref, w_ref, x_ref, gp_ref, gn_ref, xo_ref, ho_ref):
    k = pl.program_id(1)

    @pl.when(k == 0)
    def _():
        xo_ref[...] = jnp.zeros_like(xo_ref)

    xo_ref[...] += _dot(a_ref[...], w_ref[...])

    @pl.when(k == pl.num_programs(1) - 1)
    def _():
        xn = x_ref[...] + _rms(xo_ref[...], gp_ref[...])
        xo_ref[...] = xn
        ho_ref[...] = _rms(xn, gn_ref[...]).astype(ho_ref.dtype)


def matmul_res_norm(a, w, x, g_post, g_next, *, tm=1024, tk=512):
    M, K = a.shape
    N = w.shape[1]
    tm, tk = _pick(M, tm), _pick(K, tk)
    return pl.pallas_call(
        _mm_res_norm_kernel, name="matmul_res_norm",
        grid=(M // tm, K // tk),
        in_specs=[
            pl.BlockSpec((tm, tk), lambda i, k: (i, k)),
            pl.BlockSpec((tk, N), lambda i, k: (k, 0)),
            pl.BlockSpec((tm, N), lambda i, k: (i, 0), pipeline_mode=pl.Buffered(1)),
            pl.BlockSpec((1, N), lambda i, k: (0, 0)),
            pl.BlockSpec((1, N), lambda i, k: (0, 0)),
        ],
        out_specs=[pl.BlockSpec((tm, N), lambda i, k: (i, 0)), pl.BlockSpec((tm, N), lambda i, k: (i, 0))],
        out_shape=[jax.ShapeDtypeStruct((M, N), F32), jax.ShapeDtypeStruct((M, N), BF16)],
        compiler_params=_cparams(("parallel", "arbitrary")),
    )(a, w, x, g_post.reshape(1, N), g_next.reshape(1, N))


def _mla_prep_kernel(cq_ref, ckv_ref, kr_ref, krr_ref, cos_ref, sin_ref, gq_ref, gkv_ref, uq_ref, ukt_ref,
                     q_ref, kc_ref, ckvo_ref, kro_ref, *, heads, scale):
    cqn = _rms(cq_ref[...], gq_ref[...]).astype(BF16)
    q = _dot(cqn, uq_ref[...])
    cos, sin = cos_ref[...], sin_ref[...]
    hw = heads * LANES
    for h in range(heads):
        lo, hi = h * LANES, (h + 1) * LANES
        q_lat = _dot(q[:, lo:hi].astype(BF16), ukt_ref[h])
        q_rope = q[:, hw + lo:hw + hi] * cos + q[:, 2 * hw + lo:2 * hw + hi] * sin
        q_ref[h, :, 0:W_CKV] = (q_lat * scale).astype(BF16)
        q_ref[h, :, W_CKV:QK_W] = (q_rope * scale).astype(BF16)
    c = _rms(ckv_ref[...], gkv_ref[...])
    kr = kr_ref[...] * cos + krr_ref[...] * sin
    ckvo_ref[...] = c
    kro_ref[...] = kr[:, :MLA_ROPE]
    kc_ref[:, 0:W_CKV] = c.astype(BF16)
    lane = lax.broadcasted_iota(jnp.int32, kr.shape, 1)
    kc_ref[:, W_CKV:QK_W] = jnp.where(lane < MLA_ROPE, kr, 1.0).astype(BF16)


def mla_prep(proj, cos, sin, g_q, g_kv, uq_all, ukt, *, tm=256):
    M = proj.shape[0]
    heads = ukt.shape[0]
    tm = _pick(M, tm)
    scale = float((MLA_NOPE + MLA_ROPE) ** -0.5 * math.log2(math.e))
    col = lambda off, w: (lambda i: (i, off // w))
    return pl.pallas_call(
        functools.partial(_mla_prep_kernel, heads=heads, scale=scale), name="mla_prep",
        grid=(M // tm,),
        in_specs=[
            pl.BlockSpec((tm, W_CQ), col(OFF_CQ, W_CQ)),
            pl.BlockSpec((tm, W_CKV), col(OFF_CKV, W_CKV)),
            pl.BlockSpec((tm, LANES), col(OFF_KR, LANES)),
            pl.BlockSpec((tm, LANES), col(OFF_KRR, LANES)),
            pl.BlockSpec((tm, LANES), lambda i: (i, 0)),
            pl.BlockSpec((tm, LANES), lambda i: (i, 0)),
            pl.BlockSpec((1, W_CQ), lambda i: (0, 0)),
            pl.BlockSpec((1, W_CKV), lambda i: (0, 0)),
            pl.BlockSpec(uq_all.shape, lambda i: (0, 0)),
            pl.BlockSpec(ukt.shape, lambda i: (0, 0, 0)),
        ],
        out_specs=[
            pl.BlockSpec((heads, tm, QK_W), lambda i: (0, i, 0)),
            pl.BlockSpec((tm, QK_W), lambda i: (i, 0)),
            pl.BlockSpec((tm, W_CKV), lambda i: (i, 0)),
            pl.BlockSpec((tm, MLA_ROPE), lambda i: (i, 0)),
        ],
        out_shape=[
            jax.ShapeDtypeStruct((heads, M, QK_W), BF16),
            jax.ShapeDtypeStruct((M, QK_W), BF16),
            jax.ShapeDtypeStruct((M, W_CKV), F32),
            jax.ShapeDtypeStruct((M, MLA_ROPE), F32),
        ],
        compiler_params=_cparams(("parallel",)),
    )(proj, proj, proj, proj, cos, sin, g_q.reshape(1, -1), g_kv.reshape(1, -1), uq_all, ukt)


def _softmax_step(s, v, m_prev, l_prev, acc_prev):
    m_new = jnp.maximum(m_prev, jnp.max(s, axis=-1, keepdims=True))
    alpha = jnp.exp2(m_prev - m_new)
    p = jnp.exp2(s - m_new)
    l_new = alpha * l_prev + jnp.sum(p, axis=-1, keepdims=True)
    acc_new = alpha * acc_prev + _dot(p.astype(BF16), v)
    return m_new, l_new, acc_new


SOFTMAX_ROW_GROUP = 64


def _mla_prompt_kernel(q_ref, k_ref, o_ref, m_sc, l_sc, acc_sc, s_sc, p_sc, alpha_sc, *, heads, tq, tk):
    qi, ki = pl.program_id(1), pl.program_id(2)
    rows = heads * tq
    rg = SOFTMAX_ROW_GROUP
    hp = 1
    part = hp * tq
    visible = ki * tk < (qi + 1) * tq
    on_diagonal = (ki + 1) * tk > qi * tq + 1

    @pl.when(ki == 0)
    def _():
        m_sc[...] = jnp.full_like(m_sc, NEG_BIG)
        l_sc[...] = jnp.zeros_like(l_sc)
        acc_sc[...] = jnp.zeros_like(acc_sc)

    def block(masked):
        k = k_ref[...]
        if masked:
            col_minus_row = (lax.broadcasted_iota(jnp.int32, (rg, tk), 1)
                             - lax.broadcasted_iota(jnp.int32, (rg, tk), 0))

        def scores(h0):
            q = q_ref[h0:h0 + hp].reshape(part, QK_W)
            s_sc[h0 * tq:h0 * tq + part, :] = _dot_nt(q, k)

        scores(0)
        for h0 in range(0, heads, hp):
            if h0 + hp < heads:
                scores(h0 + hp)
            lo = h0 * tq
            for r0 in range(lo, lo + part, rg):
                s = s_sc[r0:r0 + rg, :]
                if masked:
                    s = jnp.where(col_minus_row <= qi * tq + (r0 % tq) - ki * tk, s, NEG_BIG)
                m_prev = m_sc[r0:r0 + rg, :]
                m_new = jnp.maximum(m_prev, jnp.max(s, axis=-1, keepdims=True))
                alpha_sc[r0:r0 + rg, :] = jnp.exp2(m_prev - m_new)
                m_sc[r0:r0 + rg, :] = m_new
                p_sc[r0:r0 + rg, :] = jnp.exp2(s - m_new).astype(BF16)
            pv = _dot(p_sc[lo:lo + part, :], k)
            alpha = alpha_sc[lo:lo + part, :]
            acc_sc[lo:lo + part, :] = alpha * acc_sc[lo:lo + part, :] + pv[:, 0:W_CKV]
            l_sc[lo:lo + part, :] = alpha * l_sc[lo:lo + part, :] + pv[:, QK_W - 1:QK_W]

    pl.when(visible & on_diagonal)(lambda: block(True))
    pl.when(visible & jnp.logical_not(on_diagonal))(lambda: block(False))

    @pl.when(ki == pl.num_programs(2) - 1)
    def _():
        o = acc_sc[...] / l_sc[...]
        o_ref[...] = o.reshape(heads, tq, W_CKV).astype(o_ref.dtype)


def mla_attn_prompt(q, kc, batch, *, tq=256, tk=512):
    heads, M, _ = q.shape
    T = M // batch
    tq, tk = _pick(T, tq), _pick(T, tk)
    assert tq % SOFTMAX_ROW_GROUP == 0
    nq, nk = T // tq, T // tk
    kmap = lambda b, qi, ki: (b * nk + jnp.minimum(ki, ((qi + 1) * tq - 1) // tk), 0)
    return pl.pallas_call(
        functools.partial(_mla_prompt_kernel, heads=heads, tq=tq, tk=tk), name="mla_attn_prompt",
        grid=(batch, nq, nk),
        in_specs=[pl.BlockSpec((heads, tq, QK_W), lambda b, qi, ki: (0, b * nq + qi, 0)),
                  pl.BlockSpec((tk, QK_W), kmap)],
        out_specs=pl.BlockSpec((heads, tq, W_CKV), lambda b, qi, ki: (0, b * nq + qi, 0)),
        out_shape=jax.ShapeDtypeStruct((heads, M, W_CKV), BF16),
        scratch_shapes=[pltpu.VMEM((heads * tq, 1), F32), pltpu.VMEM((heads * tq, 1), F32),
                        pltpu.VMEM((heads * tq, W_CKV), F32), pltpu.VMEM((heads * tq, tk), F32),
                        pltpu.VMEM((heads * tq, tk), BF16), pltpu.VMEM((heads * tq, 1), F32)],
        compiler_params=_cparams(("parallel", "parallel", "arbitrary")),
    )(q, kc)


PAGE_SLOTS = 3


def _chunk_ahead(b, c, g, n_chunks):
    c2 = c + (PAGE_SLOTS - 1)
    n = jnp.int32(n_chunks)
    return b + lax.div(c2, n), lax.rem(c2, n), lax.rem(g + PAGE_SLOTS - 1, PAGE_SLOTS)


def _page_copies(pt_ref, hbm_refs, bufs, sems, layer, b, chunk, slot, pages_per_chunk):
    copies = []
    for i in range(pages_per_chunk):
        page = pt_ref[b, chunk * pages_per_chunk + i]
        for n, (hbm, buf) in enumerate(zip(hbm_refs, bufs)):
            copies.append(pltpu.make_async_copy(hbm.at[layer, page], buf.at[slot, i], sems.at[n, slot]))
    return copies


def _mla_sample_kernel(pt_ref, q_ref, knew_ref, lat_hbm, rope_hbm, o_ref, latbuf, ropebuf, sems, knew_pad,
                       *, layer, heads, t_new, pages_per_chunk, n_chunks, page):
    b = pl.program_id(0)
    nb = pl.num_programs(0)
    rows = heads * t_new
    hbm_refs, bufs = (lat_hbm, rope_hbm), (latbuf, ropebuf)
    copies = functools.partial(_page_copies, pt_ref, hbm_refs, bufs, sems, layer,
                               pages_per_chunk=pages_per_chunk)

    @pl.when(b == 0)
    def _():
        for ahead in range(PAGE_SLOTS - 1):
            for cp in copies(ahead // n_chunks, ahead % n_chunks, ahead):
                cp.start()

    q = q_ref[0]
    q_lat, q_rope = q[:, 0:W_CKV], q[:, W_CKV:W_CKV + MLA_ROPE]
    keys = pages_per_chunk * page

    def chunk_step(c, carry):
        g = b * n_chunks + c
        slot = lax.rem(g, PAGE_SLOTS)
        nb_, nc_, nslot = _chunk_ahead(b, c, g, n_chunks)

        @pl.when(g + PAGE_SLOTS - 1 < nb * n_chunks)
        def _():
            for cp in copies(nb_, nc_, nslot):
                cp.start()

        for cp in copies(b, c, slot):
            cp.wait()
        lat = latbuf[slot].reshape(keys, W_CKV).astype(BF16)
        s_rope = jnp.concatenate([_dot(q_rope, ropebuf[slot, i].astype(BF16)) for i in range(pages_per_chunk)],
                                 axis=1)
        s = _dot_nt(q_lat, lat) + s_rope
        return _softmax_step(s, lat, *carry)

    init = (jnp.full((rows, 1), NEG_BIG, F32), jnp.zeros((rows, 1), F32), jnp.zeros((rows, W_CKV), F32))
    m, l, acc = lax.fori_loop(0, n_chunks, chunk_step, init)

    knew_pad[...] = jnp.zeros_like(knew_pad)
    knew_pad[0:t_new, :] = knew_ref[0]
    kn = knew_pad[...]
    s = _dot_nt(q, kn)
    t_of_row = lax.broadcasted_iota(jnp.int32, s.shape, 0) % t_new
    j = lax.broadcasted_iota(jnp.int32, s.shape, 1)
    s = jnp.where((j <= t_of_row) & (j < t_new), s, NEG_BIG)
    m, l, acc = _softmax_step(s, kn[:, 0:W_CKV], m, l, acc)
    o_ref[0] = (acc / l).astype(o_ref.dtype)


def mla_attn_sample(page_table, q, knew, cache_lat, cache_rope_t, layer, *, pages_per_chunk=32):
    bs, rows, _ = q.shape
    t_new = knew.shape[1]
    heads = rows // t_new
    n_pages = page_table.shape[1]
    page = cache_lat.shape[2]
    ppc = _pick(n_pages, pages_per_chunk)
    kern = functools.partial(_mla_sample_kernel, layer=layer, heads=heads, t_new=t_new,
                             pages_per_chunk=ppc, n_chunks=n_pages // ppc, page=page)
    grid_spec = pltpu.PrefetchScalarGridSpec(
        num_scalar_prefetch=1,
        grid=(bs,),
        in_specs=[
            pl.BlockSpec((1, rows, QK_W), lambda b, pt: (b, 0, 0)),
            pl.BlockSpec((1, t_new, QK_W), lambda b, pt: (b, 0, 0)),
            pl.BlockSpec(memory_space=pl.ANY),
            pl.BlockSpec(memory_space=pl.ANY),
        ],
        out_specs=pl.BlockSpec((1, rows, W_CKV), lambda b, pt: (b, 0, 0)),
        scratch_shapes=[
            pltpu.VMEM((PAGE_SLOTS, ppc, page, W_CKV), F32),
            pltpu.VMEM((PAGE_SLOTS, ppc, MLA_ROPE, page), F32),
            pltpu.SemaphoreType.DMA((2, PAGE_SLOTS)),
            pltpu.VMEM((LANES, QK_W), BF16),
        ],
    )
    return pl.pallas_call(
        kern, grid_spec=grid_spec, name="mla_attn_sample",
        out_shape=jax.ShapeDtypeStruct((bs, rows, W_CKV), BF16),
        compiler_params=_cparams(("arbitrary",)),
    )(page_table, q, knew, cache_lat, cache_rope_t)


def _mla_post_kernel(o_ref, uv_ref, g_ref, out_ref, *, heads):
    for h in range(heads):
        y = _dot(o_ref[h], uv_ref[h])
        out_ref[:, h * MLA_V:(h + 1) * MLA_V] = _rms(y, g_ref[h:h + 1, :]).astype(out_ref.dtype)


def mla_post(o_lat, uv, g_out, *, tm=512):
    heads, M, _ = o_lat.shape
    tm = _pick(M, tm)
    return pl.pallas_call(
        functools.partial(_mla_post_kernel, heads=heads), name="mla_post",
        grid=(M // tm,),
        in_specs=[pl.BlockSpec((heads, tm, W_CKV), lambda i: (0, i, 0)),
                  pl.BlockSpec(uv.shape, lambda i: (0, 0, 0)),
                  pl.BlockSpec(g_out.shape, lambda i: (0, 0))],
        out_specs=pl.BlockSpec((tm, heads * MLA_V), lambda i: (i, 0)),
        out_shape=jax.ShapeDtypeStruct((M, heads * MLA_V), BF16),
        compiler_params=_cparams(("parallel",)),
    )(o_lat, uv, g_out)


def _suffix_matrix(n):
    r = lax.broadcasted_iota(jnp.int32, (n, n), 0)
    c = lax.broadcasted_iota(jnp.int32, (n, n), 1)
    return jnp.where(r > c, 1.0, 0.0).astype(BF16)


def _suffix_sums(lk, u):
    hi = lk.astype(BF16)
    mid = (lk - hi.astype(F32)).astype(BF16)
    return _dot(hi, u) + _dot(mid, u)


def _sb_block(q, k, v, mask, u, carry, acc):
    z = _dot_nt(q, k)
    lp = _log_sigmoid(z)
    lk = lp - z
    if mask is not None:
        lk = jnp.where(mask, lk, 0.0)
    la = _suffix_sums(lk, u)
    a = jnp.exp(lp + la + carry)
    if mask is not None:
        a = jnp.where(mask, a, 0.0)
    acc = acc + _dot(a.astype(BF16), v)
    carry = carry + la[:, 0:1] + lk[:, 0:1]
    return carry, acc


def _sb_prompt_kernel(q_ref, k_ref, v_ref, g_ref, o_ref, qs_sc, carry_sc, acc_sc, *, heads, tq, tk, scale):
    qi, ki = pl.program_id(1), pl.program_id(2)
    last = ((qi + 1) * tq - 1) // tk
    rows = heads * tq

    @pl.when(ki == 0)
    def _():
        q = q_ref[...]
        for h in range(heads):
            qs_sc[h * tq:(h + 1) * tq, :] = (q[:, h * SB_D:(h + 1) * SB_D] * scale).astype(BF16)
        carry_sc[...] = jnp.zeros_like(carry_sc)
        acc_sc[...] = jnp.zeros_like(acc_sc)

    @pl.when(ki <= last)
    def _():
        kb = last - ki
        q_pos = qi * tq + (lax.broadcasted_iota(jnp.int32, (rows, tk), 0) & (tq - 1))
        k_pos = kb * tk + lax.broadcasted_iota(jnp.int32, (rows, tk), 1)
        carry, acc = _sb_block(qs_sc[...], k_ref[...].astype(BF16), v_ref[...].astype(BF16), k_pos < q_pos,
                               _suffix_matrix(tk), carry_sc[...], acc_sc[...])
        carry_sc[...], acc_sc[...] = carry, acc

    @pl.when(ki == pl.num_programs(2) - 1)
    def _():
        for h in range(heads):
            o = _rms(acc_sc[h * tq:(h + 1) * tq, :], g_ref[h:h + 1, :])
            o_ref[:, h * SB_D:(h + 1) * SB_D] = o.astype(o_ref.dtype)


def sb_attn_prompt(proj, g_out, batch, *, tq=128, tk=256):
    M = proj.shape[0]
    heads = g_out.shape[0]
    T = M // batch
    tq, tk = _pick(T, tq), _pick(T, tk)
    assert tq & (tq - 1) == 0
    nq, nk = T // tq, T // tk

    def kvmap(colblock):
        def f(b, qi, ki):
            last = ((qi + 1) * tq - 1) // tk
            return (b * nk + jnp.maximum(last - ki, 0), colblock)
        return f

    return pl.pallas_call(
        functools.partial(_sb_prompt_kernel, heads=heads, tq=tq, tk=tk, scale=float(SB_D ** -0.5)),
        name="sb_attn_prompt",
        grid=(batch, nq, nk),
        in_specs=[pl.BlockSpec((tq, W_SQ), lambda b, qi, ki: (b * nq + qi, OFF_SQ // W_SQ)),
                  pl.BlockSpec((tk, SB_D), kvmap(OFF_SK // SB_D)),
                  pl.BlockSpec((tk, SB_D), kvmap(OFF_SV // SB_D)),
                  pl.BlockSpec(g_out.shape, lambda b, qi, ki: (0, 0))],
        out_specs=pl.BlockSpec((tq, heads * SB_D), lambda b, qi, ki: (b * nq + qi, 0)),
        out_shape=jax.ShapeDtypeStruct((M, heads * SB_D), BF16),
        scratch_shapes=[pltpu.VMEM((heads * tq, SB_D), BF16), pltpu.VMEM((heads * tq, 1), F32),
                        pltpu.VMEM((heads * tq, SB_D), F32)],
        compiler_params=_cparams(("parallel", "parallel", "arbitrary")),
    )(proj, proj, proj, g_out)


def _sb_sample_kernel(pt_ref, q_ref, knew_ref, vnew_ref, g_ref, k_hbm, v_hbm, o_ref, kbuf, vbuf, sems,
                      knew_pad, vnew_pad, *, layer, t_new, pages_per_chunk, n_chunks, page, sub, scale):
    b = pl.program_id(0)
    nb = pl.num_programs(0)
    rows = q_ref.shape[1]
    copies = functools.partial(_page_copies, pt_ref, (k_hbm, v_hbm), (kbuf, vbuf), sems, layer,
                               pages_per_chunk=pages_per_chunk)
    chunk_of = lambda c: n_chunks - 1 - c

    @pl.when(b == 0)
    def _():
        for ahead in range(PAGE_SLOTS - 1):
            for cp in copies(ahead // n_chunks, chunk_of(ahead % n_chunks), ahead):
                cp.start()

    q = (q_ref[0] * scale).astype(BF16)

    knew_pad[...] = jnp.zeros_like(knew_pad)
    vnew_pad[...] = jnp.zeros_like(vnew_pad)
    knew_pad[0:t_new, :] = knew_ref[0]
    vnew_pad[0:t_new, :] = vnew_ref[0]
    t_of_row = lax.broadcasted_iota(jnp.int32, (rows, LANES), 0) % t_new
    j = lax.broadcasted_iota(jnp.int32, (rows, LANES), 1)
    carry, acc = _sb_block(q, knew_pad[...].astype(BF16), vnew_pad[...].astype(BF16), j < t_of_row,
                           _suffix_matrix(LANES), jnp.zeros((rows, 1), F32), jnp.zeros((rows, SB_D), F32))

    keys = sub * page
    u = _suffix_matrix(keys)

    def chunk_step(c, state):
        g = b * n_chunks + c
        slot = lax.rem(g, PAGE_SLOTS)
        nb_, nc_, nslot = _chunk_ahead(b, c, g, n_chunks)

        @pl.when(g + PAGE_SLOTS - 1 < nb * n_chunks)
        def _():
            for cp in copies(nb_, chunk_of(nc_), nslot):
                cp.start()

        for cp in copies(b, chunk_of(c), slot):
            cp.wait()
        carry, acc = state
        nblk = pages_per_chunk // sub
        k = kbuf[slot].reshape(nblk * keys, SB_D).astype(BF16)
        v = vbuf[slot].reshape(nblk * keys, SB_D).astype(BF16)
        z_wide = _dot_nt(q, k)
        z = jnp.concatenate([z_wide[:, i * keys:(i + 1) * keys] for i in range(nblk)], axis=0)
        lp = _log_sigmoid(z)
        lk = lp - z
        la = _suffix_sums(lk, u)
        tot = la[:, 0:1] + lk[:, 0:1]
        after = [None] * nblk
        for i in reversed(range(nblk)):
            after[i] = carry
            carry = carry + tot[i * rows:(i + 1) * rows]
        a = jnp.exp(lp + la + jnp.concatenate(after, axis=0))
        a_wide = jnp.concatenate([a[i * rows:(i + 1) * rows] for i in range(nblk)], axis=1)
        acc = acc + _dot(a_wide.astype(BF16), v)
        return carry, acc

    carry, acc = lax.fori_loop(0, n_chunks, chunk_step, (carry, acc))
    o_ref[0] = _rms(acc, g_ref[...]).astype(o_ref.dtype)


def sb_attn_sample(page_table, q, knew, vnew, g_rows, cache_k, cache_v, layer, *, pages_per_chunk=32, sub=2):
    bs, rows, _ = q.shape
    t_new = knew.shape[1]
    n_pages = page_table.shape[1]
    page = cache_k.shape[2]
    ppc = _pick(n_pages, pages_per_chunk)
    sub = _pick(ppc, sub)
    kern = functools.partial(_sb_sample_kernel, layer=layer, t_new=t_new, pages_per_chunk=ppc,
                             n_chunks=n_pages // ppc, page=page, sub=sub, scale=float(SB_D ** -0.5))
    grid_spec = pltpu.PrefetchScalarGridSpec(
        num_scalar_prefetch=1,
        grid=(bs,),
        in_specs=[
            pl.BlockSpec((1, rows, SB_D), lambda b, pt: (b, 0, 0)),
            pl.BlockSpec((1, t_new, SB_D), lambda b, pt: (b, 0, 0)),
            pl.BlockSpec((1, t_new, SB_D), lambda b, pt: (b, 0, 0)),
            pl.BlockSpec((rows, SB_D), lambda b, pt: (0, 0)),
            pl.BlockSpec(memory_space=pl.ANY),
            pl.BlockSpec(memory_space=pl.ANY),
        ],
        out_specs=pl.BlockSpec((1, rows, SB_D), lambda b, pt: (b, 0, 0)),
        scratch_shapes=[
            pltpu.VMEM((PAGE_SLOTS, ppc, page, SB_D), F32),
            pltpu.VMEM((PAGE_SLOTS, ppc, page, SB_D), F32),
            pltpu.SemaphoreType.DMA((2, PAGE_SLOTS)),
            pltpu.VMEM((LANES, SB_D), F32),
            pltpu.VMEM((LANES, SB_D), F32),
        ],
    )
    return pl.pallas_call(
        kern, grid_spec=grid_spec, name="sb_attn_sample",
        out_shape=jax.ShapeDtypeStruct((bs, rows, SB_D), BF16),
        compiler_params=_cparams(("arbitrary",)),
    )(page_table, q, knew, vnew, g_rows, cache_k, cache_v)


def _split(x):
    hi = x.astype(BF16)
    return hi, (x - hi.astype(F32)).astype(BF16)


def _split3(x):
    hi = x.astype(BF16)
    rest = x - hi.astype(F32)
    mid = rest.astype(BF16)
    return hi, mid, (rest - mid.astype(F32)).astype(BF16)


def _dot3(a, b, dims=NN_DIMS):
    ah, al = a if isinstance(a, tuple) else _split(a)
    bh, bl = b if isinstance(b, tuple) else _split(b)
    d = lambda x, y: lax.dot_general(x, y, dims, preferred_element_type=F32)
    return d(ah, bh) + d(ah, bl) + d(al, bh)


def _dn_kernel(qkv_ref, z_ref, ab_ref, prev_ref, s0_ref, wconv_ref, alog_ref, dtb_ref, gout_ref,
               o_ref, snew_ref, xe_sc, s_sc, abp_sc, zp_sc, *, heads, chunk, t_blk, rows_per_step):
    n = pl.program_id(1)
    C = chunk
    hk = heads * DN_DK
    padded = t_blk < C

    @pl.when(n == 0)
    def _():
        for g in range(rows_per_step):
            xe_sc[g, 0:SUBLANES, :] = jnp.zeros((SUBLANES, xe_sc.shape[2]), F32)
            xe_sc[g, SUBLANES - (DN_CONV - 1):SUBLANES, :] = prev_ref[g]
            s_sc[g * heads:(g + 1) * heads] = s0_ref[g]

    row = lax.broadcasted_iota(jnp.int32, (C, LANES), 0)
    lane = lax.broadcasted_iota(jnp.int32, (C, LANES), 1)
    valid = row < t_blk
    r = lax.broadcasted_iota(jnp.int32, (C, C), 0)
    c = lax.broadcasted_iota(jnp.int32, (C, C), 1)
    incl, strict = r >= c, r > c
    lower = jnp.where(incl, 1.0, 0.0).astype(BF16)
    eye = jnp.where(r == c, 1.0, 0.0)
    sel = jnp.where(lax.broadcasted_iota(jnp.int32, (SUBLANES, LANES), 0)
                    == lax.broadcasted_iota(jnp.int32, (SUBLANES, LANES), 1), 1.0, 0.0).astype(BF16)
    w = wconv_ref[...]
    rows = [_dn_row_inputs(g, qkv_ref, z_ref, ab_ref, alog_ref, dtb_ref, xe_sc, abp_sc, zp_sc, w, valid, lane,
                           lower, sel, heads=heads, C=C, t_blk=t_blk, padded=padded)
            for g in range(rows_per_step)]
    _dn_chains(rows, gout_ref, o_ref, s_sc, valid, incl, strict, eye, heads=heads, C=C, t_blk=t_blk,
               padded=padded)

    @pl.when(n == pl.num_programs(1) - 1)
    def _():
        for g in range(rows_per_step):
            snew_ref[g] = s_sc[g * heads:(g + 1) * heads]


def _dn_row_inputs(g, qkv_ref, z_ref, ab_ref, alog_ref, dtb_ref, xe_sc, abp_sc, zp_sc, w, valid, lane, lower, sel,
                   *, heads, C, t_blk, padded):
    if padded:
        xe_sc[g, SUBLANES:SUBLANES + C, :] = jnp.zeros((C, xe_sc.shape[2]), F32)
        abp_sc[g] = jnp.zeros(abp_sc.shape[1:], F32)
        zp_sc[g] = jnp.zeros(zp_sc.shape[1:], F32)
        abp_sc[g, 0:t_blk, :] = ab_ref[g]
        zp_sc[g, 0:t_blk, :] = z_ref[g]
        ab, zg = abp_sc[g], zp_sc[g]
    else:
        ab, zg = ab_ref[g], z_ref[g]
    xe_sc[g, SUBLANES:SUBLANES + t_blk, :] = qkv_ref[g]

    conv = w[DN_CONV - 1:DN_CONV, :] * xe_sc[g, SUBLANES:SUBLANES + C, :]
    for i in range(1, DN_CONV):
        conv = conv + w[DN_CONV - 1 - i:DN_CONV - i, :] * xe_sc[g, SUBLANES - i:SUBLANES - i + C, :]
    if not padded:
        xe_sc[g, 0:SUBLANES, :] = xe_sc[g, C:C + SUBLANES, :]
    act = conv * jax.nn.sigmoid(conv)

    g4 = -jnp.exp(alog_ref[...]) * _softplus(ab + dtb_ref[...])
    g4 = jnp.where(valid & (lane < heads), g4, 0.0)
    beta4 = jnp.where(valid, jax.nn.sigmoid(ab), 0.0)

    g_cum = sum(_dot(lower, part) for part in _split3(g4))
    g_cum_t = sum(_dot_nt(sel, part) for part in _split3(g_cum))
    return act, zg, beta4, g_cum, g_cum_t


def _dn_chains(rows, gout_ref, o_ref, s_sc, valid, incl, strict, eye, *, heads, C, t_blk, padded):
    hk = heads * DN_DK
    chains = [(g, h) for g in range(len(rows)) for h in range(heads)]
    q, k, v, gc, beta, decay = {}, {}, {}, {}, {}, {}
    for ch in chains:
        g, h = ch
        act, _, beta4, g_cum, g_cum_t = rows[g]
        qh = act[:, h * DN_DK:(h + 1) * DN_DK]
        kh = act[:, hk + h * DN_DK:hk + (h + 1) * DN_DK]
        vh = act[:, 2 * hk + h * DN_DV:2 * hk + (h + 1) * DN_DV]
        qh = qh * lax.rsqrt(jnp.sum(qh * qh, axis=-1, keepdims=True) + L2_EPS) * (DN_DK ** -0.5)
        kh = kh * lax.rsqrt(jnp.sum(kh * kh, axis=-1, keepdims=True) + L2_EPS)
        if padded:
            qh = jnp.where(valid, qh, 0.0)
            kh = jnp.where(valid, kh, 0.0)
            vh = jnp.where(valid, vh, 0.0)
        q[ch], k[ch], v[ch] = qh, kh, vh
        gc[ch] = g_cum[:, h:h + 1]
        beta[ch] = beta4[:, heads + h:heads + h + 1]
        decay[ch] = jnp.where(incl, jnp.exp(jnp.minimum(gc[ch] - g_cum_t[h:h + 1, :], 0.0)), 0.0)

    qkk = {ch: _dot3(jnp.concatenate([q[ch], k[ch]], axis=0), k[ch], NT_DIMS) for ch in chains}
    qk = {ch: qkk[ch][:C] * decay[ch] for ch in chains}
    a = {ch: jnp.where(strict, beta[ch] * qkk[ch][C:] * decay[ch], 0.0) for ch in chains}
    tinv = {ch: eye - a[ch] for ch in chains}
    p = {ch: _dot3(a[ch], a[ch]) for ch in chains}
    n_factors = int(math.log2(C)) - 1
    for i in range(n_factors):
        if i < n_factors - 1:
            both = {ch: _dot3(jnp.concatenate([p[ch], tinv[ch]], axis=0), _split(p[ch])) for ch in chains}
            p = {ch: both[ch][:C] for ch in chains}
            tinv = {ch: tinv[ch] + both[ch][C:] for ch in chains}
        else:
            tinv = {ch: tinv[ch] + _dot3(tinv[ch], p[ch]) for ch in chains}
    e_g = {ch: jnp.exp(gc[ch]) for ch in chains}
    uw = {ch: _dot3(tinv[ch], jnp.concatenate([beta[ch] * v[ch], (beta[ch] * e_g[ch]) * k[ch]], axis=1))
          for ch in chains}
    g_last = {ch: gc[ch][C - 1:C, :] for ch in chains}
    s = {(g, h): s_sc[g * heads + h] for g, h in chains}
    ws_qs = {ch: _dot3(jnp.concatenate([uw[ch][:, DN_DV:], q[ch] * e_g[ch]], axis=0), s[ch]) for ch in chains}
    delta = {ch: uw[ch][:, :DN_DV] - ws_qs[ch][:C] for ch in chains}
    o = {ch: ws_qs[ch][C:] + _dot3(qk[ch], delta[ch]) for ch in chains}
    s_new = {ch: jnp.exp(g_last[ch]) * s[ch]
             + _dot3(k[ch] * jnp.exp(g_last[ch] - gc[ch]), delta[ch], TN_DIMS) for ch in chains}
    for g, h in chains:
        s_sc[g * heads + h] = s_new[(g, h)]
        zh = rows[g][1][:, h * DN_DV:(h + 1) * DN_DV]
        out = _rms(o[(g, h)], gout_ref[...]) * (zh * jax.nn.sigmoid(zh))
        o_ref[g, :, h * DN_DV:(h + 1) * DN_DV] = out[0:t_blk].astype(o_ref.dtype)


def deltanet(proj4, prev_conv, s0, w_conv, a_log, dt_bias, g_out, *, chunk, rows_per_step):
    batch, n_chunks, t_blk, _ = proj4.shape
    heads = s0.shape[1]
    assert w_conv.shape[1] == W_QKV and heads * DN_DV == W_Z and 2 * heads <= SUBLANES
    G = _pick(batch, rows_per_step)
    pad = lambda v: jnp.zeros((1, LANES), F32).at[0, :heads].set(v)
    gmap = lambda colblock: (lambda b, n: (b, n, 0, colblock))
    return pl.pallas_call(
        functools.partial(_dn_kernel, heads=heads, chunk=chunk, t_blk=t_blk, rows_per_step=G), name="deltanet",
        grid=(batch // G, n_chunks),
        in_specs=[
            pl.BlockSpec((G, None, t_blk, W_QKV), gmap(OFF_QKV // W_QKV)),
            pl.BlockSpec((G, None, t_blk, W_Z), gmap(OFF_Z // W_Z)),
            pl.BlockSpec((G, None, t_blk, LANES), gmap(OFF_AB // LANES)),
            pl.BlockSpec((G, DN_CONV - 1, W_QKV), lambda b, n: (b, 0, 0)),
            pl.BlockSpec((G, heads, DN_DK, DN_DV), lambda b, n: (b, 0, 0, 0)),
            pl.BlockSpec((DN_CONV, W_QKV), lambda b, n: (0, 0)),
            pl.BlockSpec((1, LANES), lambda b, n: (0, 0)),
            pl.BlockSpec((1, LANES), lambda b, n: (0, 0)),
            pl.BlockSpec((1, DN_DV), lambda b, n: (0, 0)),
        ],
        out_specs=[
            pl.BlockSpec((G, None, t_blk, W_Z), lambda b, n: (b, n, 0, 0)),
            pl.BlockSpec((G, heads, DN_DK, DN_DV), lambda b, n: (b, 0, 0, 0)),
        ],
        out_shape=[
            jax.ShapeDtypeStruct((batch, n_chunks, t_blk, W_Z), BF16),
            jax.ShapeDtypeStruct((batch, heads, DN_DK, DN_DV), F32),
        ],
        scratch_shapes=[
            pltpu.VMEM((G, chunk + SUBLANES, W_QKV), F32),
            pltpu.VMEM((G * heads, DN_DK, DN_DV), F32),
            pltpu.VMEM((G, chunk, LANES), F32),
            pltpu.VMEM((G, chunk, W_Z), F32),
        ],
        compiler_params=_cparams(("parallel", "arbitrary")),
    )(proj4, proj4, proj4, prev_conv, s0, w_conv, pad(a_log), pad(dt_bias), g_out.reshape(1, -1))


def _pack_w_in(w, dims):
    ql, kvl, dnh, sbh = dims
    sizes = (ql, kvl, MLA_ROPE, 3 * dnh * DN_DK, dnh * DN_DV, dnh, dnh, sbh * SB_D, SB_D, SB_D)
    parts, off = [], 0
    for n in sizes:
        parts.append(w[:, off:off + n])
        off += n
    cq, ckv, kr, qkv, z, a, b, sq, sk, sv = parts
    half = MLA_ROPE // 2
    kr_rot = jnp.concatenate([-kr[:, half:], kr[:, :half]], axis=1)
    zeros = lambda n: jnp.zeros((w.shape[0], n), w.dtype)
    cols = [qkv, cq, sk, z, sq, ckv, sv, kr, zeros(LANES - MLA_ROPE), kr_rot, zeros(LANES - MLA_ROPE),
            a, b, zeros(LANES - 2 * dnh)]
    packed = jnp.concatenate(cols, axis=1)
    assert packed.shape[1] == NP_IN, packed.shape
    return packed.astype(BF16)


def _pack_uq(uq):
    ql, heads, _ = uq.shape
    half = MLA_ROPE // 2
    nope = uq[:, :, :MLA_NOPE].reshape(ql, heads * MLA_NOPE)
    rope = uq[:, :, MLA_NOPE:]
    rot = jnp.concatenate([-rope[:, :, half:], rope[:, :, :half]], axis=2)
    padz = jnp.zeros((ql, heads, LANES - MLA_ROPE), uq.dtype)
    rope_p = jnp.concatenate([rope, padz], axis=2).reshape(ql, heads * LANES)
    rot_p = jnp.concatenate([rot, padz], axis=2).reshape(ql, heads * LANES)
    return jnp.concatenate([nope, rope_p, rot_p], axis=1).astype(BF16)


def _rope_tables(pos):
    half = MLA_ROPE // 2
    inv_freq = ROPE_THETA ** (-jnp.arange(half, dtype=F32) / half)
    ang = pos.astype(F32)[:, None] * inv_freq[None, :]
    z = jnp.zeros((pos.shape[0], LANES - MLA_ROPE), F32)
    cos = jnp.concatenate([jnp.cos(ang), jnp.cos(ang), z], axis=1)
    sin = jnp.concatenate([jnp.sin(ang), jnp.sin(ang), z], axis=1)
    return cos, sin


def _layer(x, h, lw, group):
    M = x.shape[0]
    batch, t_len = group["batch"], group["t"]
    proj = matmul(h, lw["w_in"], F32, tm=1024, tn=1280)
    q, kc, ckv_o, kr_o = mla_prep(proj, group["cos"], group["sin"], lw["g_mla_q"], lw["g_mla_kv"],
                                  lw["uq"], lw["ukt"])
    heads = q.shape[0]
    sbh = lw["g_sb_out"].shape[0]
    if group["paged"] is None:
        o_lat = mla_attn_prompt(q, kc, batch)
        o_sb = sb_attn_prompt(proj, lw["g_sb_out"], batch)
        n_chunks, chunk, dn_rows = t_len // DN_CHUNK, DN_CHUNK, 2
        prev_conv = jnp.zeros((batch, DN_CONV - 1, W_QKV), F32)
        s0 = jnp.zeros((batch, W_Z // DN_DV, DN_DK, DN_DV), F32)
    else:
        pg = group["paged"]
        layer = pg["layer"]
        to_rows = lambda a, nh: a.reshape(nh, batch, t_len, a.shape[-1]).transpose(1, 0, 2, 3).reshape(
            batch, nh * t_len, a.shape[-1])
        from_rows = lambda a, nh: a.reshape(batch, nh, t_len, a.shape[-1]).transpose(1, 0, 2, 3).reshape(
            nh, M, a.shape[-1])
        o_rows = mla_attn_sample(pg["page_table"], to_rows(q, heads), kc.reshape(batch, t_len, QK_W),
                                 pg["lat"], pg["rope"], layer)
        o_lat = from_rows(o_rows, heads)
        sq = proj[:, OFF_SQ:OFF_SQ + W_SQ].reshape(batch, t_len, sbh, SB_D).transpose(0, 2, 1, 3)
        sk = proj[:, OFF_SK:OFF_SK + SB_D].reshape(batch, t_len, SB_D)
        sv = proj[:, OFF_SV:OFF_SV + SB_D].reshape(batch, t_len, SB_D)
        g_rows = jnp.repeat(lw["g_sb_out"], t_len, axis=0)
        o_sb_rows = sb_attn_sample(pg["page_table"], sq.reshape(batch, sbh * t_len, SB_D), sk, sv, g_rows,
                                   pg["sbk"], pg["sbv"], layer)
        o_sb = o_sb_rows.reshape(batch, sbh, t_len, SB_D).transpose(0, 2, 1, 3).reshape(M, sbh * SB_D)
        n_chunks, chunk, dn_rows = 1, SUBLANES, 4
        assert t_len <= chunk
        prev_conv, s0 = pg["conv"], pg["S"]
    o_mla = mla_post(o_lat, lw["uv"], lw["g_mla_out"])
    o_dn, s_new = deltanet(proj.reshape(batch, n_chunks, t_len // n_chunks, NP_IN), prev_conv, s0, lw["w_conv"],
                           lw["a_log"], lw["dt_bias"], lw["g_dn_out"], chunk=chunk, rows_per_step=dn_rows)
    mixed = jnp.concatenate([o_mla, o_dn.reshape(M, W_Z), o_sb], axis=1)
    x1, hm = matmul_res_norm(mixed, lw["w_out"], x, lw["g_post_mix"], lw["g_pre_mlp"])
    ff = matmul(hm, lw["w_up"], BF16, relu2=True, tm=1024, tn=1024)
    x2, h_next = matmul_res_norm(ff, lw["w_down"], x1, lw["g_post_mlp"], lw["g_next"])
    proj_b = proj.reshape(batch, t_len, NP_IN)
    state = (ckv_o.reshape(batch, t_len, W_CKV), kr_o.reshape(batch, t_len, MLA_ROPE),
             proj_b[:, :, OFF_SK:OFF_SK + SB_D], proj_b[:, :, OFF_SV:OFF_SV + SB_D], s_new,
             proj_b[:, t_len - (DN_CONV - 1):, OFF_QKV:OFF_QKV + W_QKV])
    return x2, h_next, state


def kernel(x_prompt, x_sample, cache_mla_latent, cache_mla_rope, cache_sb_k, cache_sb_v, state_dn_S, state_dn_conv, page_table, w_in, g_pre_mix, g_mla_q, w_mla_uq, g_mla_kv, w_mla_uk, w_mla_uv, g_mla_out, w_dn_conv, dn_A_log, dn_dt_bias, g_dn_out, g_sb_out, w_out, g_post_mix, g_pre_mlp, w_up, w_down, g_post_mlp):
    depth = w_in.shape[0]
    B, T, D = x_prompt.shape
    Bs, Ts, _ = x_sample.shape
    past_len = page_table.shape[1] * cache_mla_latent.shape[2]
    dims = (w_mla_uq.shape[1], w_mla_uk.shape[1], dn_A_log.shape[1], g_sb_out.shape[1])
    assert dims == (W_CQ, W_CKV, W_Z // DN_DV, W_SQ // SB_D) and T % DN_CHUNK == 0

    rope_t = jnp.swapaxes(cache_mla_rope, 2, 3)
    cos_p, sin_p = _rope_tables(jnp.tile(jnp.arange(T), B))
    cos_s, sin_s = _rope_tables(jnp.tile(past_len + jnp.arange(Ts), Bs))
    groups = [dict(batch=B, t=T, cos=cos_p, sin=sin_p, paged=None),
              dict(batch=Bs, t=Ts, cos=cos_s, sin=sin_s, paged=None)]
    xs = [x_prompt.reshape(B * T, D), x_sample.reshape(Bs * Ts, D)]
    hs = [norm_cast(x, g_pre_mix[0]) for x in xs]
    states = [[], []]
    for l in range(depth):
        lw = dict(
            w_in=_pack_w_in(w_in[l], dims), g_mla_q=g_mla_q[l], g_mla_kv=g_mla_kv[l],
            uq=_pack_uq(w_mla_uq[l]), ukt=w_mla_uk[l].transpose(1, 2, 0).astype(BF16),
            uv=w_mla_uv[l].transpose(1, 0, 2).astype(BF16), g_mla_out=g_mla_out[l],
            w_conv=w_dn_conv[l], a_log=dn_A_log[l], dt_bias=dn_dt_bias[l], g_dn_out=g_dn_out[l],
            g_sb_out=g_sb_out[l], w_out=w_out[l].astype(BF16), g_post_mix=g_post_mix[l],
            g_pre_mlp=g_pre_mlp[l], w_up=w_up[l].astype(BF16), w_down=w_down[l].astype(BF16),
            g_post_mlp=g_post_mlp[l], g_next=g_pre_mix[(l + 1) % depth])
        groups[1]["paged"] = dict(layer=l, page_table=page_table, lat=cache_mla_latent, rope=rope_t,
                                  sbk=cache_sb_k, sbv=cache_sb_v, S=state_dn_S[l], conv=state_dn_conv[l])
        for gi in range(2):
            xs[gi], hs[gi], st = _layer(xs[gi], hs[gi], lw, groups[gi])
            states[gi].append(st)
    p_state = tuple(jnp.stack(t) for t in zip(*states[0]))
    s_state = tuple(jnp.stack(t) for t in zip(*states[1]))
    return (xs[0].reshape(B, T, D), xs[1].reshape(Bs, Ts, D)) + p_state + s_state
```

```python
import functools
import math

import jax
import jax.numpy as jnp
from jax import lax
from jax.experimental import pallas as pl
from jax.experimental.pallas import tpu as pltpu

F32 = jnp.float32
BF16 = jnp.bfloat16

MLA_NOPE = 128
MLA_ROPE = 64
MLA_V = 128
DN_DK = 128
DN_DV = 128
DN_CONV = 4
DN_CHUNK = 64
SB_D = 128
NORM_EPS = 1e-6
L2_EPS = 1e-6
NEG_BIG = -1e30
ROPE_THETA = 10000.0

LANES = 128
SUBLANES = 8
VMEM_LIMIT = 48 * 1024 * 1024

OFF_QKV, W_QKV = 0, 1536
OFF_CQ, W_CQ = 1536, 384
OFF_SK = 1920
OFF_Z, W_Z = 2048, 512
OFF_SQ, W_SQ = 2560, 512
OFF_CKV, W_CKV = 3072, 256
OFF_SV = 3328
OFF_KR = 3456
OFF_KRR = 3584
OFF_AB = 3712
NP_IN = 3840
QK_W = 384

NN_DIMS = (((1,), (0,)), ((), ()))
NT_DIMS = (((1,), (1,)), ((), ()))
TN_DIMS = (((0,), (0,)), ((), ()))


def _cparams(sem):
    return pltpu.CompilerParams(dimension_semantics=sem, vmem_limit_bytes=VMEM_LIMIT)


def _rms(x, g):
    return x * lax.rsqrt(jnp.mean(x * x, axis=-1, keepdims=True) + NORM_EPS) * g


def _dot(a, b):
    return jnp.dot(a, b, preferred_element_type=F32)


def _dot_nt(a, b):
    return lax.dot_general(a, b, NT_DIMS, preferred_element_type=F32)


def _log_sigmoid(z):
    return jnp.minimum(z, 0.0) - jnp.log1p(jnp.exp(-jnp.abs(z)))


def _softplus(x):
    return jnp.maximum(x, 0.0) + jnp.log1p(jnp.exp(-jnp.abs(x)))


def _pick(total, want):
    t = min(total, want)
    assert total % t == 0, (total, want)
    return t


def _norm_cast_kernel(x_ref, g_ref, o_ref):
    o_ref[...] = _rms(x_ref[...], g_ref[...]).astype(o_ref.dtype)


def norm_cast(x, g):
    M, D = x.shape
    tm = _pick(M, 512)
    return pl.pallas_call(
        _norm_cast_kernel, name="norm_cast",
        grid=(M // tm,),
        in_specs=[pl.BlockSpec((tm, D), lambda i: (i, 0)), pl.BlockSpec((1, D), lambda i: (0, 0))],
        out_specs=pl.BlockSpec((tm, D), lambda i: (i, 0)),
        out_shape=jax.ShapeDtypeStruct((M, D), BF16),
        compiler_params=_cparams(("parallel",)),
    )(x, g.reshape(1, D))


def _mm_kernel(a_ref, w_ref, o_ref, *, relu2):
    acc = _dot(a_ref[...], w_ref[...])
    if relu2:
        acc = jnp.square(jnp.maximum(acc, 0.0))
    o_ref[...] = acc.astype(o_ref.dtype)


def matmul(a, w, out_dtype, *, relu2=False, tm=1024, tn=1024):
    M, K = a.shape
    N = w.shape[1]
    tm, tn = _pick(M, tm), _pick(N, tn)
    return pl.pallas_call(
        functools.partial(_mm_kernel, relu2=relu2), name="matmul_relu2" if relu2 else "matmul",
        grid=(M // tm, N // tn),
        in_specs=[pl.BlockSpec((tm, K), lambda i, j: (i, 0)), pl.BlockSpec((K, tn), lambda i, j: (0, j))],
        out_specs=pl.BlockSpec((tm, tn), lambda i, j: (i, j)),
        out_shape=jax.ShapeDtypeStruct((M, N), out_dtype),
        compiler_params=_cparams(("parallel", "arbitrary")),
    )(a, w)


def _mm_res_norm_kernel(a_ref, w_ref, x_ref, gp_ref, gn_ref, xo_ref, ho_ref, acc_ref):
    k = pl.program_id(1)

    @pl.when(k == 0)
    def _():
        acc_ref[...] = jnp.zeros_like(acc_ref)

    acc_ref[...] += _dot(a_ref[...], w_ref[...])

    @pl.when(k == pl.num_programs(1) - 1)
    def _():
        xn = x_ref[...] + _rms(acc_ref[...], gp_ref[...])
        xo_ref[...] = xn
        ho_ref[...] = _rms(xn, gn_ref[...]).astype(ho_ref.dtype)


def matmul_res_norm(a, w, x, g_post, g_next, *, tm=512, tk=1024):
    M, K = a.shape
    N = w.shape[1]
    tm, tk = _pick(M, tm), _pick(K, tk)
    return pl.pallas_call(
        _mm_res_norm_kernel, name="matmul_res_norm",
        grid=(M // tm, K // tk),
        in_specs=[
            pl.BlockSpec((tm, tk), lambda i, k: (i, k)),
            pl.BlockSpec((tk, N), lambda i, k: (k, 0)),
            pl.BlockSpec((tm, N), lambda i, k: (i, 0)),
            pl.BlockSpec((1, N), lambda i, k: (0, 0)),
            pl.BlockSpec((1, N), lambda i, k: (0, 0)),
        ],
        out_specs=[pl.BlockSpec((tm, N), lambda i, k: (i, 0)), pl.BlockSpec((tm, N), lambda i, k: (i, 0))],
        out_shape=[jax.ShapeDtypeStruct((M, N), F32), jax.ShapeDtypeStruct((M, N), BF16)],
        scratch_shapes=[pltpu.VMEM((tm, N), F32)],
        compiler_params=_cparams(("parallel", "arbitrary")),
    )(a, w, x, g_post.reshape(1, N), g_next.reshape(1, N))


def _mla_prep_kernel(cq_ref, ckv_ref, kr_ref, krr_ref, cos_ref, sin_ref, gq_ref, gkv_ref, uq_ref, ukt_ref,
                     q_ref, kc_ref, ckvo_ref, kro_ref, *, heads, scale):
    cqn = _rms(cq_ref[...], gq_ref[...]).astype(BF16)
    q = _dot(cqn, uq_ref[...])
    cos, sin = cos_ref[...], sin_ref[...]
    hw = heads * LANES
    for h in range(heads):
        lo, hi = h * LANES, (h + 1) * LANES
        q_lat = _dot(q[:, lo:hi].astype(BF16), ukt_ref[h])
        q_rope = q[:, hw + lo:hw + hi] * cos + q[:, 2 * hw + lo:2 * hw + hi] * sin
        q_ref[h, :, 0:W_CKV] = (q_lat * scale).astype(BF16)
        q_ref[h, :, W_CKV:QK_W] = (q_rope * scale).astype(BF16)
    c = _rms(ckv_ref[...], gkv_ref[...])
    kr = kr_ref[...] * cos + krr_ref[...] * sin
    ckvo_ref[...] = c
    kro_ref[...] = kr[:, :MLA_ROPE]
    kc_ref[:, 0:W_CKV] = c.astype(BF16)
    lane = lax.broadcasted_iota(jnp.int32, kr.shape, 1)
    kc_ref[:, W_CKV:QK_W] = jnp.where(lane < MLA_ROPE, kr, 1.0).astype(BF16)


def mla_prep(proj, cos, sin, g_q, g_kv, uq_all, ukt, *, tm=256):
    M = proj.shape[0]
    heads = ukt.shape[0]
    tm = _pick(M, tm)
    scale = float((MLA_NOPE + MLA_ROPE) ** -0.5 * math.log2(math.e))
    col = lambda off, w: (lambda i: (i, off // w))
    return pl.pallas_call(
        functools.partial(_mla_prep_kernel, heads=heads, scale=scale), name="mla_prep",
        grid=(M // tm,),
        in_specs=[
            pl.BlockSpec((tm, W_CQ), col(OFF_CQ, W_CQ)),
            pl.BlockSpec((tm, W_CKV), col(OFF_CKV, W_CKV)),
            pl.BlockSpec((tm, LANES), col(OFF_KR, LANES)),
            pl.BlockSpec((tm, LANES), col(OFF_KRR, LANES)),
            pl.BlockSpec((tm, LANES), lambda i: (i, 0)),
            pl.BlockSpec((tm, LANES), lambda i: (i, 0)),
            pl.BlockSpec((1, W_CQ), lambda i: (0, 0)),
            pl.BlockSpec((1, W_CKV), lambda i: (0, 0)),
            pl.BlockSpec(uq_all.shape, lambda i: (0, 0)),
            pl.BlockSpec(ukt.shape, lambda i: (0, 0, 0)),
        ],
        out_specs=[
            pl.BlockSpec((heads, tm, QK_W), lambda i: (0, i, 0)),
            pl.BlockSpec((tm, QK_W), lambda i: (i, 0)),
            pl.BlockSpec((tm, W_CKV), lambda i: (i, 0)),
            pl.BlockSpec((tm, MLA_ROPE), lambda i: (i, 0)),
        ],
        out_shape=[
            jax.ShapeDtypeStruct((heads, M, QK_W), BF16),
            jax.ShapeDtypeStruct((M, QK_W), BF16),
            jax.ShapeDtypeStruct((M, W_CKV), F32),
            jax.ShapeDtypeStruct((M, MLA_ROPE), F32),
        ],
        compiler_params=_cparams(("parallel",)),
    )(proj, proj, proj, proj, cos, sin, g_q.reshape(1, -1), g_kv.reshape(1, -1), uq_all, ukt)


def _softmax_step(s, v, m_prev, l_prev, acc_prev):
    m_new = jnp.maximum(m_prev, jnp.max(s, axis=-1, keepdims=True))
    alpha = jnp.exp2(m_prev - m_new)
    p = jnp.exp2(s - m_new)
    l_new = alpha * l_prev + jnp.sum(p, axis=-1, keepdims=True)
    acc_new = alpha * acc_prev + _dot(p.astype(BF16), v)
    return m_new, l_new, acc_new


SOFTMAX_ROW_GROUP = 64


def _mla_prompt_kernel(q_ref, k_ref, o_ref, m_sc, l_sc, acc_sc, s_sc, p_sc, alpha_sc, *, heads, tq, tk):
    qi, ki = pl.program_id(1), pl.program_id(2)
    rows = heads * tq
    rg = SOFTMAX_ROW_GROUP
    hp = 1
    part = hp * tq
    visible = ki * tk < (qi + 1) * tq
    on_diagonal = (ki + 1) * tk > qi * tq + 1

    @pl.when(ki == 0)
    def _():
        m_sc[...] = jnp.full_like(m_sc, NEG_BIG)
        l_sc[...] = jnp.zeros_like(l_sc)
        acc_sc[...] = jnp.zeros_like(acc_sc)

    def block(masked):
        k = k_ref[...]
        if masked:
            col_minus_row = (lax.broadcasted_iota(jnp.int32, (rg, tk), 1)
                             - lax.broadcasted_iota(jnp.int32, (rg, tk), 0))

        def scores(h0):
            q = q_ref[h0:h0 + hp].reshape(part, QK_W)
            s_sc[h0 * tq:h0 * tq + part, :] = _dot_nt(q, k)

        scores(0)
        for h0 in range(0, heads, hp):
            if h0 + hp < heads:
                scores(h0 + hp)
            lo = h0 * tq
            for r0 in range(lo, lo + part, rg):
                s = s_sc[r0:r0 + rg, :]
                if masked:
                    s = jnp.where(col_minus_row <= qi * tq + (r0 % tq) - ki * tk, s, NEG_BIG)
                m_prev = m_sc[r0:r0 + rg, :]
                m_new = jnp.maximum(m_prev, jnp.max(s, axis=-1, keepdims=True))
                alpha_sc[r0:r0 + rg, :] = jnp.exp2(m_prev - m_new)
                m_sc[r0:r0 + rg, :] = m_new
                p_sc[r0:r0 + rg, :] = jnp.exp2(s - m_new).astype(BF16)
            pv = _dot(p_sc[lo:lo + part, :], k)
            alpha = alpha_sc[lo:lo + part, :]
            acc_sc[lo:lo + part, :] = alpha * acc_sc[lo:lo + part, :] + pv[:, 0:W_CKV]
            l_sc[lo:lo + part, :] = alpha * l_sc[lo:lo + part, :] + pv[:, QK_W - 1:QK_W]

    pl.when(visible & on_diagonal)(lambda: block(True))
    pl.when(visible & jnp.logical_not(on_diagonal))(lambda: block(False))

    @pl.when(ki == pl.num_programs(2) - 1)
    def _():
        o = acc_sc[...] / l_sc[...]
        o_ref[...] = o.reshape(heads, tq, W_CKV).astype(o_ref.dtype)


def mla_attn_prompt(q, kc, batch, *, tq=256, tk=512):
    heads, M, _ = q.shape
    T = M // batch
    tq, tk = _pick(T, tq), _pick(T, tk)
    assert tq % SOFTMAX_ROW_GROUP == 0
    nq, nk = T // tq, T // tk
    kmap = lambda b, qi, ki: (b * nk + jnp.minimum(ki, ((qi + 1) * tq - 1) // tk), 0)
    return pl.pallas_call(
        functools.partial(_mla_prompt_kernel, heads=heads, tq=tq, tk=tk), name="mla_attn_prompt",
        grid=(batch, nq, nk),
        in_specs=[pl.BlockSpec((heads, tq, QK_W), lambda b, qi, ki: (0, b * nq + qi, 0)),
                  pl.BlockSpec((tk, QK_W), kmap)],
        out_specs=pl.BlockSpec((heads, tq, W_CKV), lambda b, qi, ki: (0, b * nq + qi, 0)),
        out_shape=jax.ShapeDtypeStruct((heads, M, W_CKV), BF16),
        scratch_shapes=[pltpu.VMEM((heads * tq, 1), F32), pltpu.VMEM((heads * tq, 1), F32),
                        pltpu.VMEM((heads * tq, W_CKV), F32), pltpu.VMEM((heads * tq, tk), F32),
                        pltpu.VMEM((heads * tq, tk), BF16), pltpu.VMEM((heads * tq, 1), F32)],
        compiler_params=_cparams(("parallel", "parallel", "arbitrary")),
    )(q, kc)


PAGE_SLOTS = 3


def _chunk_ahead(b, c, g, n_chunks):
    c2 = c + (PAGE_SLOTS - 1)
    n = jnp.int32(n_chunks)
    return b + lax.div(c2, n), lax.rem(c2, n), lax.rem(g + PAGE_SLOTS - 1, PAGE_SLOTS)


def _page_copies(pt_ref, hbm_refs, bufs, sems, layer, b, chunk, slot, pages_per_chunk):
    copies = []
    for i in range(pages_per_chunk):
        page = pt_ref[b, chunk * pages_per_chunk + i]
        for n, (hbm, buf) in enumerate(zip(hbm_refs, bufs)):
            copies.append(pltpu.make_async_copy(hbm.at[layer, page], buf.at[slot, i], sems.at[n, slot]))
    return copies


def _mla_sample_kernel(pt_ref, q_ref, knew_ref, lat_hbm, rope_hbm, o_ref, latbuf, ropebuf, sems, knew_pad,
                       *, layer, heads, t_new, pages_per_chunk, n_chunks, page):
    b = pl.program_id(0)
    nb = pl.num_programs(0)
    rows = heads * t_new
    hbm_refs, bufs = (lat_hbm, rope_hbm), (latbuf, ropebuf)
    copies = functools.partial(_page_copies, pt_ref, hbm_refs, bufs, sems, layer,
                               pages_per_chunk=pages_per_chunk)

    @pl.when(b == 0)
    def _():
        for ahead in range(PAGE_SLOTS - 1):
            for cp in copies(ahead // n_chunks, ahead % n_chunks, ahead):
                cp.start()

    q = q_ref[0]
    q_lat, q_rope = q[:, 0:W_CKV], q[:, W_CKV:W_CKV + MLA_ROPE]
    part_pages = max(1, pages_per_chunk // 4)

    def chunk_step(c, carry):
        g = b * n_chunks + c
        slot = lax.rem(g, PAGE_SLOTS)
        nb_, nc_, nslot = _chunk_ahead(b, c, g, n_chunks)

        @pl.when(g + PAGE_SLOTS - 1 < nb * n_chunks)
        def _():
            for cp in copies(nb_, nc_, nslot):
                cp.start()

        for cp in copies(b, c, slot):
            cp.wait()
        parts = []
        for p0 in range(0, pages_per_chunk, part_pages):
            lat = latbuf[slot, p0:p0 + part_pages].reshape(part_pages * page, W_CKV).astype(BF16)
            s_rope = jnp.concatenate([_dot(q_rope, ropebuf[slot, i].astype(BF16))
                                      for i in range(p0, p0 + part_pages)], axis=1)
            parts.append((_dot_nt(q_lat, lat) + s_rope, lat))
        for s, lat in parts:
            carry = _softmax_step(s, lat, *carry)
        return carry

    init = (jnp.full((rows, 1), NEG_BIG, F32), jnp.zeros((rows, 1), F32), jnp.zeros((rows, W_CKV), F32))
    m, l, acc = lax.fori_loop(0, n_chunks, chunk_step, init)

    knew_pad[...] = jnp.zeros_like(knew_pad)
    knew_pad[0:t_new, :] = knew_ref[0]
    kn = knew_pad[...]
    s = _dot_nt(q, kn)
    t_of_row = lax.broadcasted_iota(jnp.int32, s.shape, 0) % t_new
    j = lax.broadcasted_iota(jnp.int32, s.shape, 1)
    s = jnp.where((j <= t_of_row) & (j < t_new), s, NEG_BIG)
    m, l, acc = _softmax_step(s, kn[:, 0:W_CKV], m, l, acc)
    o_ref[0] = (acc / l).astype(o_ref.dtype)


def mla_attn_sample(page_table, q, knew, cache_lat, cache_rope_t, layer, *, pages_per_chunk=32):
    bs, rows, _ = q.shape
    t_new = knew.shape[1]
    heads = rows // t_new
    n_pages = page_table.shape[1]
    page = cache_lat.shape[2]
    ppc = _pick(n_pages, pages_per_chunk)
    kern = functools.partial(_mla_sample_kernel, layer=layer, heads=heads, t_new=t_new,
                             pages_per_chunk=ppc, n_chunks=n_pages // ppc, page=page)
    grid_spec = pltpu.PrefetchScalarGridSpec(
        num_scalar_prefetch=1,
        grid=(bs,),
        in_specs=[
            pl.BlockSpec((1, rows, QK_W), lambda b, pt: (b, 0, 0)),
            pl.BlockSpec((1, t_new, QK_W), lambda b, pt: (b, 0, 0)),
            pl.BlockSpec(memory_space=pl.ANY),
            pl.BlockSpec(memory_space=pl.ANY),
        ],
        out_specs=pl.BlockSpec((1, rows, W_CKV), lambda b, pt: (b, 0, 0)),
        scratch_shapes=[
            pltpu.VMEM((PAGE_SLOTS, ppc, page, W_CKV), F32),
            pltpu.VMEM((PAGE_SLOTS, ppc, MLA_ROPE, page), F32),
            pltpu.SemaphoreType.DMA((2, PAGE_SLOTS)),
            pltpu.VMEM((LANES, QK_W), BF16),
        ],
    )
    return pl.pallas_call(
        kern, grid_spec=grid_spec, name="mla_attn_sample",
        out_shape=jax.ShapeDtypeStruct((bs, rows, W_CKV), BF16),
        compiler_params=_cparams(("arbitrary",)),
    )(page_table, q, knew, cache_lat, cache_rope_t)


def _mla_post_kernel(o_ref, uv_ref, g_ref, out_ref, *, heads):
    for h in range(heads):
        y = _dot(o_ref[h], uv_ref[h])
        out_ref[:, h * MLA_V:(h + 1) * MLA_V] = _rms(y, g_ref[h:h + 1, :]).astype(out_ref.dtype)


def mla_post(o_lat, uv, g_out, *, tm=512):
    heads, M, _ = o_lat.shape
    tm = _pick(M, tm)
    return pl.pallas_call(
        functools.partial(_mla_post_kernel, heads=heads), name="mla_post",
        grid=(M // tm,),
        in_specs=[pl.BlockSpec((heads, tm, W_CKV), lambda i: (0, i, 0)),
                  pl.BlockSpec(uv.shape, lambda i: (0, 0, 0)),
                  pl.BlockSpec(g_out.shape, lambda i: (0, 0))],
        out_specs=pl.BlockSpec((tm, heads * MLA_V), lambda i: (i, 0)),
        out_shape=jax.ShapeDtypeStruct((M, heads * MLA_V), BF16),
        compiler_params=_cparams(("parallel",)),
    )(o_lat, uv, g_out)


def _suffix_matrix(n):
    r = lax.broadcasted_iota(jnp.int32, (n, n), 0)
    c = lax.broadcasted_iota(jnp.int32, (n, n), 1)
    return jnp.where(r > c, 1.0, 0.0).astype(BF16)


def _suffix_sums(lk, u):
    hi = lk.astype(BF16)
    mid = (lk - hi.astype(F32)).astype(BF16)
    return _dot(hi, u) + _dot(mid, u)


def _sb_block(q, k, v, mask, u, carry, acc):
    z = _dot_nt(q, k)
    lp = _log_sigmoid(z)
    lk = lp - z
    if mask is not None:
        lk = jnp.where(mask, lk, 0.0)
    la = _suffix_sums(lk, u)
    a = jnp.exp(lp + la + carry)
    if mask is not None:
        a = jnp.where(mask, a, 0.0)
    acc = acc + _dot(a.astype(BF16), v)
    carry = carry + la[:, 0:1] + lk[:, 0:1]
    return carry, acc


def _sb_prompt_kernel(q_ref, k_ref, v_ref, g_ref, o_ref, qs_sc, carry_sc, acc_sc, *, heads, tq, tk, scale):
    qi, ki = pl.program_id(1), pl.program_id(2)
    last = ((qi + 1) * tq - 1) // tk
    rows = heads * tq

    @pl.when(ki == 0)
    def _():
        q = q_ref[...]
        for h in range(heads):
            qs_sc[h * tq:(h + 1) * tq, :] = (q[:, h * SB_D:(h + 1) * SB_D] * scale).astype(BF16)
        carry_sc[...] = jnp.zeros_like(carry_sc)
        acc_sc[...] = jnp.zeros_like(acc_sc)

    kb = last - ki
    on_diagonal = (kb + 1) * tk > qi * tq

    def block(masked):
        k, v, u = k_ref[...].astype(BF16), v_ref[...].astype(BF16), _suffix_matrix(tk)
        mask = None
        if masked:
            q_pos = qi * tq + lax.broadcasted_iota(jnp.int32, (tq, tk), 0)
            k_pos = kb * tk + lax.broadcasted_iota(jnp.int32, (tq, tk), 1)
            mask = k_pos < q_pos
        hs = [slice(h * tq, (h + 1) * tq) for h in range(heads)]
        z = [_dot_nt(qs_sc[r, :], k) for r in hs]
        lp = [_log_sigmoid(x) for x in z]
        lk = [p - x for p, x in zip(lp, z)]
        if masked:
            lk = [jnp.where(mask, x, 0.0) for x in lk]
        la = [_suffix_sums(x, u) for x in lk]
        a = [jnp.exp(p + s + carry_sc[r, :]) for p, s, r in zip(lp, la, hs)]
        if masked:
            a = [jnp.where(mask, x, 0.0) for x in a]
        for r, x, s, y in zip(hs, a, la, lk):
            acc_sc[r, :] += _dot(x.astype(BF16), v)
            carry_sc[r, :] += s[:, 0:1] + y[:, 0:1]

    pl.when((ki <= last) & on_diagonal)(lambda: block(True))
    pl.when((ki <= last) & jnp.logical_not(on_diagonal))(lambda: block(False))

    @pl.when(ki == pl.num_programs(2) - 1)
    def _():
        for h in range(heads):
            o = _rms(acc_sc[h * tq:(h + 1) * tq, :], g_ref[h:h + 1, :])
            o_ref[:, h * SB_D:(h + 1) * SB_D] = o.astype(o_ref.dtype)


def sb_attn_prompt(proj, g_out, batch, *, tq=256, tk=256):
    M = proj.shape[0]
    heads = g_out.shape[0]
    T = M // batch
    tq, tk = _pick(T, tq), _pick(T, tk)
    assert tq & (tq - 1) == 0
    nq, nk = T // tq, T // tk

    def kvmap(colblock):
        def f(b, qi, ki):
            last = ((qi + 1) * tq - 1) // tk
            return (b * nk + jnp.maximum(last - ki, 0), colblock)
        return f

    return pl.pallas_call(
        functools.partial(_sb_prompt_kernel, heads=heads, tq=tq, tk=tk, scale=float(SB_D ** -0.5)),
        name="sb_attn_prompt",
        grid=(batch, nq, nk),
        in_specs=[pl.BlockSpec((tq, W_SQ), lambda b, qi, ki: (b * nq + qi, OFF_SQ // W_SQ)),
                  pl.BlockSpec((tk, SB_D), kvmap(OFF_SK // SB_D)),
                  pl.BlockSpec((tk, SB_D), kvmap(OFF_SV // SB_D)),
                  pl.BlockSpec(g_out.shape, lambda b, qi, ki: (0, 0))],
        out_specs=pl.BlockSpec((tq, heads * SB_D), lambda b, qi, ki: (b * nq + qi, 0)),
        out_shape=jax.ShapeDtypeStruct((M, heads * SB_D), BF16),
        scratch_shapes=[pltpu.VMEM((heads * tq, SB_D), BF16), pltpu.VMEM((heads * tq, 1), F32),
                        pltpu.VMEM((heads * tq, SB_D), F32)],
        compiler_params=_cparams(("parallel", "parallel", "arbitrary")),
    )(proj, proj, proj, g_out)


def _sb_sample_kernel(pt_ref, q_ref, knew_ref, vnew_ref, g_ref, k_hbm, v_hbm, o_ref, kbuf, vbuf, sems,
                      knew_pad, vnew_pad, *, layer, t_new, pages_per_chunk, n_chunks, page, sub, scale):
    b = pl.program_id(0)
    nb = pl.num_programs(0)
    rows = q_ref.shape[1]
    copies = functools.partial(_page_copies, pt_ref, (k_hbm, v_hbm), (kbuf, vbuf), sems, layer,
                               pages_per_chunk=pages_per_chunk)
    chunk_of = lambda c: n_chunks - 1 - c

    @pl.when(b == 0)
    def _():
        for ahead in range(PAGE_SLOTS - 1):
            for cp in copies(ahead // n_chunks, chunk_of(ahead % n_chunks), ahead):
                cp.start()

    q = (q_ref[0] * scale).astype(BF16)

    knew_pad[...] = jnp.zeros_like(knew_pad)
    vnew_pad[...] = jnp.zeros_like(vnew_pad)
    knew_pad[0:t_new, :] = knew_ref[0]
    vnew_pad[0:t_new, :] = vnew_ref[0]
    t_of_row = lax.broadcasted_iota(jnp.int32, (rows, LANES), 0) % t_new
    j = lax.broadcasted_iota(jnp.int32, (rows, LANES), 1)
    carry, acc = _sb_block(q, knew_pad[...].astype(BF16), vnew_pad[...].astype(BF16), j < t_of_row,
                           _suffix_matrix(LANES), jnp.zeros((rows, 1), F32), jnp.zeros((rows, SB_D), F32))

    keys = sub * page
    u = _suffix_matrix(keys)

    def chunk_step(c, state):
        g = b * n_chunks + c
        slot = lax.rem(g, PAGE_SLOTS)
        nb_, nc_, nslot = _chunk_ahead(b, c, g, n_chunks)

        @pl.when(g + PAGE_SLOTS - 1 < nb * n_chunks)
        def _():
            for cp in copies(nb_, chunk_of(nc_), nslot):
                cp.start()

        for cp in copies(b, chunk_of(c), slot):
            cp.wait()
        carry, acc = state
        nblk = pages_per_chunk // sub
        k = kbuf[slot].reshape(nblk * keys, SB_D).astype(BF16)
        v = vbuf[slot].reshape(nblk * keys, SB_D).astype(BF16)
        z_wide = _dot_nt(q, k)
        z = jnp.concatenate([z_wide[:, i * keys:(i + 1) * keys] for i in range(nblk)], axis=0)
        lp = _log_sigmoid(z)
        lk = lp - z
        la = _suffix_sums(lk, u)
        tot = la[:, 0:1] + lk[:, 0:1]
        after = [None] * nblk
        for i in reversed(range(nblk)):
            after[i] = carry
            carry = carry + tot[i * rows:(i + 1) * rows]
        a = jnp.exp(lp + la + jnp.concatenate(after, axis=0))
        a_wide = jnp.concatenate([a[i * rows:(i + 1) * rows] for i in range(nblk)], axis=1)
        acc = acc + _dot(a_wide.astype(BF16), v)
        return carry, acc

    carry, acc = lax.fori_loop(0, n_chunks, chunk_step, (carry, acc))
    o_ref[0] = _rms(acc, g_ref[...]).astype(o_ref.dtype)


def sb_attn_sample(page_table, q, knew, vnew, g_rows, cache_k, cache_v, layer, *, pages_per_chunk=32, sub=2):
    bs, rows, _ = q.shape
    t_new = knew.shape[1]
    n_pages = page_table.shape[1]
    page = cache_k.shape[2]
    ppc = _pick(n_pages, pages_per_chunk)
    sub = _pick(ppc, sub)
    kern = functools.partial(_sb_sample_kernel, layer=layer, t_new=t_new, pages_per_chunk=ppc,
                             n_chunks=n_pages // ppc, page=page, sub=sub, scale=float(SB_D ** -0.5))
    grid_spec = pltpu.PrefetchScalarGridSpec(
        num_scalar_prefetch=1,
        grid=(bs,),
        in_specs=[
            pl.BlockSpec((1, rows, SB_D), lambda b, pt: (b, 0, 0)),
            pl.BlockSpec((1, t_new, SB_D), lambda b, pt: (b, 0, 0)),
            pl.BlockSpec((1, t_new, SB_D), lambda b, pt: (b, 0, 0)),
            pl.BlockSpec((rows, SB_D), lambda b, pt: (0, 0)),
            pl.BlockSpec(memory_space=pl.ANY),
            pl.BlockSpec(memory_space=pl.ANY),
        ],
        out_specs=pl.BlockSpec((1, rows, SB_D), lambda b, pt: (b, 0, 0)),
        scratch_shapes=[
            pltpu.VMEM((PAGE_SLOTS, ppc, page, SB_D), F32),
            pltpu.VMEM((PAGE_SLOTS, ppc, page, SB_D), F32),
            pltpu.SemaphoreType.DMA((2, PAGE_SLOTS)),
            pltpu.VMEM((LANES, SB_D), F32),
            pltpu.VMEM((LANES, SB_D), F32),
        ],
    )
    return pl.pallas_call(
        kern, grid_spec=grid_spec, name="sb_attn_sample",
        out_shape=jax.ShapeDtypeStruct((bs, rows, SB_D), BF16),
        compiler_params=_cparams(("arbitrary",)),
    )(page_table, q, knew, vnew, g_rows, cache_k, cache_v)


def _split(x):
    hi = x.astype(BF16)
    return hi, (x - hi.astype(F32)).astype(BF16)


def _split3(x):
    hi = x.astype(BF16)
    rest = x - hi.astype(F32)
    mid = rest.astype(BF16)
    return hi, mid, (rest - mid.astype(F32)).astype(BF16)


def _dot3(a, b, dims=NN_DIMS):
    ah, al = a if isinstance(a, tuple) else _split(a)
    bh, bl = b if isinstance(b, tuple) else _split(b)
    d = lambda x, y: lax.dot_general(x, y, dims, preferred_element_type=F32)
    return d(ah, bh) + d(ah, bl) + d(al, bh)


def _dn_kernel(qkv_ref, z_ref, ab_ref, prev_ref, s0_ref, wconv_ref, alog_ref, dtb_ref, gout_ref,
               o_ref, snew_ref, xe_sc, s_sc, abp_sc, zp_sc, *, heads, chunk, t_blk, rows_per_step):
    n = pl.program_id(1)
    C = chunk
    hk = heads * DN_DK
    padded = t_blk < C

    @pl.when(n == 0)
    def _():
        for g in range(rows_per_step):
            xe_sc[g, 0:SUBLANES, :] = jnp.zeros((SUBLANES, xe_sc.shape[2]), F32)
            xe_sc[g, SUBLANES - (DN_CONV - 1):SUBLANES, :] = prev_ref[g]
            s_sc[g * heads:(g + 1) * heads] = s0_ref[g]

    row = lax.broadcasted_iota(jnp.int32, (C, LANES), 0)
    lane = lax.broadcasted_iota(jnp.int32, (C, LANES), 1)
    valid = row < t_blk
    r = lax.broadcasted_iota(jnp.int32, (C, C), 0)
    c = lax.broadcasted_iota(jnp.int32, (C, C), 1)
    incl, strict = r >= c, r > c
    lower = jnp.where(incl, 1.0, 0.0).astype(BF16)
    eye = jnp.where(r == c, 1.0, 0.0)
    sel = jnp.where(lax.broadcasted_iota(jnp.int32, (SUBLANES, LANES), 0)
                    == lax.broadcasted_iota(jnp.int32, (SUBLANES, LANES), 1), 1.0, 0.0).astype(BF16)
    w = wconv_ref[...]
    rows = [_dn_row_inputs(g, qkv_ref, z_ref, ab_ref, alog_ref, dtb_ref, xe_sc, abp_sc, zp_sc, w, valid, lane,
                           lower, sel, heads=heads, C=C, t_blk=t_blk, padded=padded)
            for g in range(rows_per_step)]
    _dn_chains(rows, gout_ref, o_ref, s_sc, valid, incl, strict, eye, heads=heads, C=C, t_blk=t_blk,
               padded=padded)

    @pl.when(n == pl.num_programs(1) - 1)
    def _():
        for g in range(rows_per_step):
            snew_ref[g] = s_sc[g * heads:(g + 1) * heads]


def _dn_row_inputs(g, qkv_ref, z_ref, ab_ref, alog_ref, dtb_ref, xe_sc, abp_sc, zp_sc, w, valid, lane, lower, sel,
                   *, heads, C, t_blk, padded):
    if padded:
        xe_sc[g, SUBLANES:SUBLANES + C, :] = jnp.zeros((C, xe_sc.shape[2]), F32)
        abp_sc[g] = jnp.zeros(abp_sc.shape[1:], F32)
        zp_sc[g] = jnp.zeros(zp_sc.shape[1:], F32)
        abp_sc[g, 0:t_blk, :] = ab_ref[g]
        zp_sc[g, 0:t_blk, :] = z_ref[g]
        ab, zg = abp_sc[g], zp_sc[g]
    else:
        ab, zg = ab_ref[g], z_ref[g]
    xe_sc[g, SUBLANES:SUBLANES + t_blk, :] = qkv_ref[g]

    conv = w[DN_CONV - 1:DN_CONV, :] * xe_sc[g, SUBLANES:SUBLANES + C, :]
    for i in range(1, DN_CONV):
        conv = conv + w[DN_CONV - 1 - i:DN_CONV - i, :] * xe_sc[g, SUBLANES - i:SUBLANES - i + C, :]
    if not padded:
        xe_sc[g, 0:SUBLANES, :] = xe_sc[g, C:C + SUBLANES, :]
    act = conv * jax.nn.sigmoid(conv)

    g4 = -jnp.exp(alog_ref[...]) * _softplus(ab + dtb_ref[...])
    g4 = jnp.where(valid & (lane < heads), g4, 0.0)
    beta4 = jnp.where(valid, jax.nn.sigmoid(ab), 0.0)

    g_cum = sum(_dot(lower, part) for part in _split3(g4))
    g_cum_t = sum(_dot_nt(sel, part) for part in _split3(g_cum))
    return act, zg, beta4, g_cum, g_cum_t


def _dn_chains(rows, gout_ref, o_ref, s_sc, valid, incl, strict, eye, *, heads, C, t_blk, padded):
    hk = heads * DN_DK
    chains = [(g, h) for g in range(len(rows)) for h in range(heads)]
    q, k, v, gc, beta, decay = {}, {}, {}, {}, {}, {}
    for ch in chains:
        g, h = ch
        act, _, beta4, g_cum, g_cum_t = rows[g]
        qh = act[:, h * DN_DK:(h + 1) * DN_DK]
        kh = act[:, hk + h * DN_DK:hk + (h + 1) * DN_DK]
        vh = act[:, 2 * hk + h * DN_DV:2 * hk + (h + 1) * DN_DV]
        qh = qh * lax.rsqrt(jnp.sum(qh * qh, axis=-1, keepdims=True) + L2_EPS) * (DN_DK ** -0.5)
        kh = kh * lax.rsqrt(jnp.sum(kh * kh, axis=-1, keepdims=True) + L2_EPS)
        if padded:
            qh = jnp.where(valid, qh, 0.0)
            kh = jnp.where(valid, kh, 0.0)
            vh = jnp.where(valid, vh, 0.0)
        q[ch], k[ch], v[ch] = qh, kh, vh
        gc[ch] = g_cum[:, h:h + 1]
        beta[ch] = beta4[:, heads + h:heads + h + 1]
        decay[ch] = jnp.where(incl, jnp.exp(jnp.minimum(gc[ch] - g_cum_t[h:h + 1, :], 0.0)), 0.0)

    qkk = {ch: _dot3(jnp.concatenate([q[ch], k[ch]], axis=0), k[ch], NT_DIMS) for ch in chains}
    qk = {ch: qkk[ch][:C] * decay[ch] for ch in chains}
    a = {ch: jnp.where(strict, beta[ch] * qkk[ch][C:] * decay[ch], 0.0) for ch in chains}
    tinv = {ch: eye - a[ch] for ch in chains}
    p = {ch: _dot3(a[ch], a[ch]) for ch in chains}
    n_factors = int(math.log2(C)) - 1
    for i in range(n_factors):
        if i < n_factors - 1:
            both = {ch: _dot3(jnp.concatenate([p[ch], tinv[ch]], axis=0), _split(p[ch])) for ch in chains}
            p = {ch: both[ch][:C] for ch in chains}
            tinv = {ch: tinv[ch] + both[ch][C:] for ch in chains}
        else:
            tinv = {ch: tinv[ch] + _dot3(tinv[ch], p[ch]) for ch in chains}
    e_g = {ch: jnp.exp(gc[ch]) for ch in chains}
    uw = {ch: _dot3(tinv[ch], jnp.concatenate([beta[ch] * v[ch], (beta[ch] * e_g[ch]) * k[ch]], axis=1))
          for ch in chains}
    g_last = {ch: gc[ch][C - 1:C, :] for ch in chains}
    s = {(g, h): s_sc[g * heads + h] for g, h in chains}
    ws_qs = {ch: _dot3(jnp.concatenate([uw[ch][:, DN_DV:], q[ch] * e_g[ch]], axis=0), s[ch]) for ch in chains}
    delta = {ch: uw[ch][:, :DN_DV] - ws_qs[ch][:C] for ch in chains}
    o = {ch: ws_qs[ch][C:] + _dot3(qk[ch], delta[ch]) for ch in chains}
    s_new = {ch: jnp.exp(g_last[ch]) * s[ch]
             + _dot3(k[ch] * jnp.exp(g_last[ch] - gc[ch]), delta[ch], TN_DIMS) for ch in chains}
    for g, h in chains:
        s_sc[g * heads + h] = s_new[(g, h)]
        zh = rows[g][1][:, h * DN_DV:(h + 1) * DN_DV]
        out = _rms(o[(g, h)], gout_ref[...]) * (zh * jax.nn.sigmoid(zh))
        o_ref[g, :, h * DN_DV:(h + 1) * DN_DV] = out[0:t_blk].astype(o_ref.dtype)


def deltanet(proj4, prev_conv, s0, w_conv, a_log, dt_bias, g_out, *, chunk, rows_per_step):
    batch, n_chunks, t_blk, _ = proj4.shape
    heads = s0.shape[1]
    assert w_conv.shape[1] == W_QKV and heads * DN_DV == W_Z and 2 * heads <= SUBLANES
    G = _pick(batch, rows_per_step)
    pad = lambda v: jnp.zeros((1, LANES), F32).at[0, :heads].set(v)
    gmap = lambda colblock: (lambda b, n: (b, n, 0, colblock))
    return pl.pallas_call(
        functools.partial(_dn_kernel, heads=heads, chunk=chunk, t_blk=t_blk, rows_per_step=G), name="deltanet",
        grid=(batch // G, n_chunks),
        in_specs=[
            pl.BlockSpec((G, None, t_blk, W_QKV), gmap(OFF_QKV // W_QKV)),
            pl.BlockSpec((G, None, t_blk, W_Z), gmap(OFF_Z // W_Z)),
            pl.BlockSpec((G, None, t_blk, LANES), gmap(OFF_AB // LANES)),
            pl.BlockSpec((G, DN_CONV - 1, W_QKV), lambda b, n: (b, 0, 0)),
            pl.BlockSpec((G, heads, DN_DK, DN_DV), lambda b, n: (b, 0, 0, 0)),
            pl.BlockSpec((DN_CONV, W_QKV), lambda b, n: (0, 0)),
            pl.BlockSpec((1, LANES), lambda b, n: (0, 0)),
            pl.BlockSpec((1, LANES), lambda b, n: (0, 0)),
            pl.BlockSpec((1, DN_DV), lambda b, n: (0, 0)),
        ],
        out_specs=[
            pl.BlockSpec((G, None, t_blk, W_Z), lambda b, n: (b, n, 0, 0)),
            pl.BlockSpec((G, heads, DN_DK, DN_DV), lambda b, n: (b, 0, 0, 0)),
        ],
        out_shape=[
            jax.ShapeDtypeStruct((batch, n_chunks, t_blk, W_Z), BF16),
            jax.ShapeDtypeStruct((batch, heads, DN_DK, DN_DV), F32),
        ],
        scratch_shapes=[
            pltpu.VMEM((G, chunk + SUBLANES, W_QKV), F32),
            pltpu.VMEM((G * heads, DN_DK, DN_DV), F32),
            pltpu.VMEM((G, chunk, LANES), F32),
            pltpu.VMEM((G, chunk, W_Z), F32),
        ],
        compiler_params=_cparams(("parallel", "arbitrary")),
    )(proj4, proj4, proj4, prev_conv, s0, w_conv, pad(a_log), pad(dt_bias), g_out.reshape(1, -1))


def _pack_w_in(w, dims):
    ql, kvl, dnh, sbh = dims
    sizes = (ql, kvl, MLA_ROPE, 3 * dnh * DN_DK, dnh * DN_DV, dnh, dnh, sbh * SB_D, SB_D, SB_D)
    parts, off = [], 0
    for n in sizes:
        parts.append(w[:, off:off + n])
        off += n
    cq, ckv, kr, qkv, z, a, b, sq, sk, sv = parts
    half = MLA_ROPE // 2
    kr_rot = jnp.concatenate([-kr[:, half:], kr[:, :half]], axis=1)
    zeros = lambda n: jnp.zeros((w.shape[0], n), w.dtype)
    cols = [qkv, cq, sk, z, sq, ckv, sv, kr, zeros(LANES - MLA_ROPE), kr_rot, zeros(LANES - MLA_ROPE),
            a, b, zeros(LANES - 2 * dnh)]
    packed = jnp.concatenate(cols, axis=1)
    assert packed.shape[1] == NP_IN, packed.shape
    return packed.astype(BF16)


def _pack_uq(uq):
    ql, heads, _ = uq.shape
    half = MLA_ROPE // 2
    nope = uq[:, :, :MLA_NOPE].reshape(ql, heads * MLA_NOPE)
    rope = uq[:, :, MLA_NOPE:]
    rot = jnp.concatenate([-rope[:, :, half:], rope[:, :, :half]], axis=2)
    padz = jnp.zeros((ql, heads, LANES - MLA_ROPE), uq.dtype)
    rope_p = jnp.concatenate([rope, padz], axis=2).reshape(ql, heads * LANES)
    rot_p = jnp.concatenate([rot, padz], axis=2).reshape(ql, heads * LANES)
    return jnp.concatenate([nope, rope_p, rot_p], axis=1).astype(BF16)


def _rope_tables(pos):
    half = MLA_ROPE // 2
    inv_freq = ROPE_THETA ** (-jnp.arange(half, dtype=F32) / half)
    ang = pos.astype(F32)[:, None] * inv_freq[None, :]
    z = jnp.zeros((pos.shape[0], LANES - MLA_ROPE), F32)
    cos = jnp.concatenate([jnp.cos(ang), jnp.cos(ang), z], axis=1)
    sin = jnp.concatenate([jnp.sin(ang), jnp.sin(ang), z], axis=1)
    return cos, sin


def _layer(x, h, lw, group):
    M = x.shape[0]
    batch, t_len = group["batch"], group["t"]
    proj = matmul(h, lw["w_in"], F32, tm=1024, tn=1280)
    q, kc, ckv_o, kr_o = mla_prep(proj, group["cos"], group["sin"], lw["g_mla_q"], lw["g_mla_kv"],
                                  lw["uq"], lw["ukt"])
    heads = q.shape[0]
    sbh = lw["g_sb_out"].shape[0]
    if group["paged"] is None:
        o_lat = mla_attn_prompt(q, kc, batch)
        o_sb = sb_attn_prompt(proj, lw["g_sb_out"], batch)
        n_chunks, chunk, dn_rows = t_len // DN_CHUNK, DN_CHUNK, 2
        prev_conv = jnp.zeros((batch, DN_CONV - 1, W_QKV), F32)
        s0 = jnp.zeros((batch, W_Z // DN_DV, DN_DK, DN_DV), F32)
    else:
        pg = group["paged"]
        layer = pg["layer"]
        to_rows = lambda a, nh: a.reshape(nh, batch, t_len, a.shape[-1]).transpose(1, 0, 2, 3).reshape(
            batch, nh * t_len, a.shape[-1])
        from_rows = lambda a, nh: a.reshape(batch, nh, t_len, a.shape[-1]).transpose(1, 0, 2, 3).reshape(
            nh, M, a.shape[-1])
        o_rows = mla_attn_sample(pg["page_table"], to_rows(q, heads), kc.reshape(batch, t_len, QK_W),
                                 pg["lat"], pg["rope"], layer)
        o_lat = from_rows(o_rows, heads)
        sq = proj[:, OFF_SQ:OFF_SQ + W_SQ].reshape(batch, t_len, sbh, SB_D).transpose(0, 2, 1, 3)
        sk = proj[:, OFF_SK:OFF_SK + SB_D].reshape(batch, t_len, SB_D)
        sv = proj[:, OFF_SV:OFF_SV + SB_D].reshape(batch, t_len, SB_D)
        g_rows = jnp.repeat(lw["g_sb_out"], t_len, axis=0)
        o_sb_rows = sb_attn_sample(pg["page_table"], sq.reshape(batch, sbh * t_len, SB_D), sk, sv, g_rows,
                                   pg["sbk"], pg["sbv"], layer)
        o_sb = o_sb_rows.reshape(batch, sbh, t_len, SB_D).transpose(0, 2, 1, 3).reshape(M, sbh * SB_D)
        n_chunks, chunk, dn_rows = 1, SUBLANES, 4
        assert t_len <= chunk
        prev_conv, s0 = pg["conv"], pg["S"]
    o_mla = mla_post(o_lat, lw["uv"], lw["g_mla_out"])
    o_dn, s_new = deltanet(proj.reshape(batch, n_chunks, t_len // n_chunks, NP_IN), prev_conv, s0, lw["w_conv"],
                           lw["a_log"], lw["dt_bias"], lw["g_dn_out"], chunk=chunk, rows_per_step=dn_rows)
    mixed = jnp.concatenate([o_mla, o_dn.reshape(M, W_Z), o_sb], axis=1)
    x1, hm = matmul_res_norm(mixed, lw["w_out"], x, lw["g_post_mix"], lw["g_pre_mlp"])
    ff = matmul(hm, lw["w_up"], BF16, relu2=True, tm=1024, tn=1024)
    x2, h_next = matmul_res_norm(ff, lw["w_down"], x1, lw["g_post_mlp"], lw["g_next"])
    proj_b = proj.reshape(batch, t_len, NP_IN)
    state = (ckv_o.reshape(batch, t_len, W_CKV), kr_o.reshape(batch, t_len, MLA_ROPE),
             proj_b[:, :, OFF_SK:OFF_SK + SB_D], proj_b[:, :, OFF_SV:OFF_SV + SB_D], s_new,
             proj_b[:, t_len - (DN_CONV - 1):, OFF_QKV:OFF_QKV + W_QKV])
    return x2, h_next, state


def kernel(x_prompt, x_sample, cache_mla_latent, cache_mla_rope, cache_sb_k, cache_sb_v, state_dn_S, state_dn_conv, page_table, w_in, g_pre_mix, g_mla_q, w_mla_uq, g_mla_kv, w_mla_uk, w_mla_uv, g_mla_out, w_dn_conv, dn_A_log, dn_dt_bias, g_dn_out, g_sb_out, w_out, g_post_mix, g_pre_mlp, w_up, w_down, g_post_mlp):
    depth = w_in.shape[0]
    B, T, D = x_prompt.shape
    Bs, Ts, _ = x_sample.shape
    past_len = page_table.shape[1] * cache_mla_latent.shape[2]
    dims = (w_mla_uq.shape[1], w_mla_uk.shape[1], dn_A_log.shape[1], g_sb_out.shape[1])
    assert dims == (W_CQ, W_CKV, W_Z // DN_DV, W_SQ // SB_D) and T % DN_CHUNK == 0

    rope_t = jnp.swapaxes(cache_mla_rope, 2, 3)
    cos_p, sin_p = _rope_tables(jnp.tile(jnp.arange(T), B))
    cos_s, sin_s = _rope_tables(jnp.tile(past_len + jnp.arange(Ts), Bs))
    groups = [dict(batch=B, t=T, cos=cos_p, sin=sin_p, paged=None),
              dict(batch=Bs, t=Ts, cos=cos_s, sin=sin_s, paged=None)]
    xs = [x_prompt.reshape(B * T, D), x_sample.reshape(Bs * Ts, D)]
    hs = [norm_cast(x, g_pre_mix[0]) for x in xs]
    states = [[], []]
    for l in range(depth):
        lw = dict(
            w_in=_pack_w_in(w_in[l], dims), g_mla_q=g_mla_q[l], g_mla_kv=g_mla_kv[l],
            uq=_pack_uq(w_mla_uq[l]), ukt=w_mla_uk[l].transpose(1, 2, 0).astype(BF16),
            uv=w_mla_uv[l].transpose(1, 0, 2).astype(BF16), g_mla_out=g_mla_out[l],
            w_conv=w_dn_conv[l], a_log=dn_A_log[l], dt_bias=dn_dt_bias[l], g_dn_out=g_dn_out[l],
            g_sb_out=g_sb_out[l], w_out=w_out[l].astype(BF16), g_post_mix=g_post_mix[l],
            g_pre_mlp=g_pre_mlp[l], w_up=w_up[l].astype(BF16), w_down=w_down[l].astype(BF16),
            g_post_mlp=g_post_mlp[l], g_next=g_pre_mix[(l + 1) % depth])
        groups[1]["paged"] = dict(layer=l, page_table=page_table, lat=cache_mla_latent, rope=rope_t,
                                  sbk=cache_sb_k, sbv=cache_sb_v, S=state_dn_S[l], conv=state_dn_conv[l])
        for gi in range(2):
            xs[gi], hs[gi], st = _layer(xs[gi], hs[gi], lw, groups[gi])
            states[gi].append(st)
    p_state = tuple(jnp.stack(t) for t in zip(*states[0]))
    s_state = tuple(jnp.stack(t) for t in zip(*states[1]))
    return (xs[0].reshape(B, T, D), xs[1].reshape(Bs, Ts, D)) + p_state + s_state
```
